```python
import math
import jax, jax.numpy as jnp
from jax import lax
import numpy as np

D_MODEL = 1024
BATCH = 16
SEQ = 2048
DEPTH = 1
DEC_BATCH = 16
DEC_SEQ = 4096
PAST_LEN = 128

HEAD_DIM = 64
N_HEADS = D_MODEL // HEAD_DIM
ATTN_HEADS = N_HEADS // 2
HGRN_HEADS = N_HEADS - ATTN_HEADS
ATTN_WIDTH = ATTN_HEADS * HEAD_DIM
HGRN_KEY_DIM = HEAD_DIM
HGRN_VAL_DIM = HEAD_DIM
HGRN_KEY_WIDTH = HGRN_HEADS * HGRN_KEY_DIM
HGRN_VAL_WIDTH = HGRN_HEADS * HGRN_VAL_DIM
MIX_WIDTH = ATTN_WIDTH + HGRN_VAL_WIDTH
IN_SIZES = (ATTN_WIDTH, ATTN_WIDTH, ATTN_WIDTH,
            HGRN_KEY_WIDTH, HGRN_KEY_WIDTH, HGRN_KEY_WIDTH,
            HGRN_VAL_WIDTH, HGRN_VAL_WIDTH)
IN_WIDTH = sum(IN_SIZES)
DILATED_BRANCHES = ((128, 1), (512, 4), (2048, 16))
ROPE_THETA = 500000.0
ROT_DIM = HEAD_DIM // 4
HGRN_CHUNK = 64
D_FF = ((8 * D_MODEL // 3 + 255) // 256) * 256
CONV_WIDTH = 3
NORM_EPS = 1e-6
NEG_INF = -1e30

kernel_name = "hybrid_dilated_attn_hgrn2_encoder"


def _rmsnorm(x, w):
    xf = x.astype(jnp.float32)
    y = xf * lax.rsqrt(jnp.mean(xf * xf, axis=-1, keepdims=True) + NORM_EPS)
    return (y * w.astype(jnp.float32)).astype(x.dtype)


def _partial_rotary(t, positions):
    half = ROT_DIM // 2
    inv_freq = ROPE_THETA ** (-jnp.arange(half, dtype=jnp.float32) * 2.0 / ROT_DIM)
    ang = positions.astype(jnp.float32)[:, None] * inv_freq[None, :]
    cos = jnp.cos(ang)[None, :, None, :]
    sin = jnp.sin(ang)[None, :, None, :]
    tr = t[..., :ROT_DIM].astype(jnp.float32)
    x1, x2 = tr[..., :half], tr[..., half:]
    rot = jnp.concatenate([x1 * cos - x2 * sin, x2 * cos + x1 * sin], axis=-1)
    return jnp.concatenate([rot.astype(t.dtype), t[..., ROT_DIM:]], axis=-1)


def _dilated_branch(q, k, v, window, dilation):
    B, S, H, Dh = q.shape
    n_side = window // (2 * dilation)
    blk = n_side
    L = S // dilation
    nb = -(-L // blk)
    Lp = nb * blk

    def to_sub(t):
        return t.reshape(B, L, dilation, H, Dh).transpose(0, 2, 3, 1, 4)

    qs, ks, vs = to_sub(q), to_sub(k), to_sub(v)
    qb = jnp.pad(qs, ((0, 0), (0, 0), (0, 0), (0, Lp - L), (0, 0))).reshape(B, dilation, H, nb, blk, Dh)

    def neighbours(t):
        tb = jnp.pad(t, ((0, 0), (0, 0), (0, 0), (blk, Lp - L + blk), (0, 0))).reshape(B, dilation, H, nb + 2, blk, Dh)
        return jnp.concatenate([tb[:, :, :, :-2], tb[:, :, :, 1:-1], tb[:, :, :, 2:]], axis=4)

    kb, vb = neighbours(ks), neighbours(vs)
    s = jnp.einsum('brhnqd,brhnkd->brhnqk', qb, kb, preferred_element_type=jnp.float32) * (1.0 / math.sqrt(Dh))
    qi = jnp.arange(blk)[:, None]
    ki = jnp.arange(3 * blk)[None, :]
    band = jnp.abs(qi + blk - ki) <= n_side
    kpos = jnp.arange(nb)[:, None] * blk + jnp.arange(3 * blk)[None, :] - blk
    inrange = (kpos >= 0) & (kpos < L)
    mask = band[None, :, :] & inrange[:, None, :]
    s = jnp.where(mask, s, NEG_INF)
    lse = jax.nn.logsumexp(s, axis=-1)
    p = jnp.exp(s - lse[..., None])
    o = jnp.einsum('brhnqk,brhnkd->brhnqd', p.astype(v.dtype), vb)
    o = o.reshape(B, dilation, H, Lp, Dh)[:, :, :, :L].transpose(0, 3, 1, 2, 4).reshape(B, S, H, Dh)
    lse = lse.reshape(B, dilation, H, Lp)[:, :, :, :L].transpose(0, 3, 1, 2).reshape(B, S, H)
    return o, lse


def _dilated_attention(q, k, v):
    outs, lses = [], []
    for window, dilation in DILATED_BRANCHES:
        o, lse = _dilated_branch(q, k, v, window, dilation)
        outs.append(o)
        lses.append(lse)
    w = jax.nn.softmax(jnp.stack(lses, axis=0), axis=0)
    return jnp.einsum('nbsh,nbshd->bshd', w.astype(q.dtype), jnp.stack(outs, axis=0))


def _hgrn2_scan(q, k, v, log_f):
    B, S, H, Dk = q.shape
    Dv = v.shape[-1]
    C = HGRN_CHUNK
    n = S // C

    def chunks(t):
        return t.reshape(B, n, C, H, t.shape[-1]).transpose(1, 0, 3, 2, 4)

    causal = jnp.tril(jnp.ones((C, C), dtype=bool))

    def step(state, inp):
        qc, kc, vc, gc = inp
        A = jnp.cumsum(gc, axis=-2)
        o_inter = jnp.einsum('bhck,bhkv->bhcv', qc * jnp.exp(A), state)
        diff = A[:, :, :, None, :] - A[:, :, None, :, :]
        decay = jnp.exp(jnp.where(causal[:, :, None], diff, -jnp.inf))
        att = jnp.einsum('bhtk,bhtsk,bhsk->bhts', qc, decay, kc)
        o_intra = jnp.einsum('bhts,bhsv->bhtv', att, vc)
        A_last = A[:, :, -1:, :]
        new_state = jnp.exp(A_last[:, :, 0, :])[..., None] * state + jnp.einsum(
            'bhsk,bhsv->bhkv', kc * jnp.exp(A_last - A), vc)
        return new_state, o_inter + o_intra

    init = jnp.zeros((B, H, Dk, Dv), jnp.float32)
    _, o = lax.scan(step, init, (chunks(q), chunks(k), chunks(v), chunks(log_f)))
    return o.transpose(1, 0, 3, 2, 4).reshape(B, S, H, Dv)


def _hgrn2_gate(z, lb):
    zf = z.astype(jnp.float32)
    log_f = jnp.log(lb + (1.0 - lb) * jax.nn.sigmoid(zf))
    k = (1.0 - lb) * jax.nn.sigmoid(-zf)
    return log_f, k


def _mixer(xn, w_in, lb_fwd_param, lb_bwd_param, out_norm_w, w_out, layer):
    B, S, _ = xn.shape
    proj = xn @ w_in
    offs = [int(o) for o in np.cumsum(IN_SIZES)[:-1]]
    aq, ak, av, hq, hf_f, hf_b, hi, hg = jnp.split(proj, offs, axis=-1)

    pos = jnp.arange(S)
    aq = _partial_rotary(aq.reshape(B, S, ATTN_HEADS, HEAD_DIM), pos)
    ak = _partial_rotary(ak.reshape(B, S, ATTN_HEADS, HEAD_DIM), pos)
    av = av.reshape(B, S, ATTN_HEADS, HEAD_DIM)
    attn_out = _dilated_attention(aq, ak, av).reshape(B, S, ATTN_WIDTH)

    lb_f = jnp.cumsum(jax.nn.softmax(lb_fwd_param.astype(jnp.float32), axis=0), axis=0)[layer]
    lb_b = jnp.cumsum(jax.nn.softmax(lb_bwd_param.astype(jnp.float32), axis=0), axis=0)[layer]
    lb_f = lb_f.reshape(HGRN_HEADS, HGRN_KEY_DIM)
    lb_b = lb_b.reshape(HGRN_HEADS, HGRN_KEY_DIM)
    q = jax.nn.silu(hq.astype(jnp.float32)).reshape(B, S, HGRN_HEADS, HGRN_KEY_DIM)
    v = hi.astype(jnp.float32).reshape(B, S, HGRN_HEADS, HGRN_VAL_DIM)
    logf_f, k_f = _hgrn2_gate(hf_f.reshape(B, S, HGRN_HEADS, HGRN_KEY_DIM), lb_f)
    logf_b, k_b = _hgrn2_gate(hf_b.reshape(B, S, HGRN_HEADS, HGRN_KEY_DIM), lb_b)
    o_fwd = _hgrn2_scan(q, k_f, v, logf_f)
    flip = lambda t: jnp.flip(t, axis=1)
    o_bwd = flip(_hgrn2_scan(flip(q), flip(k_b), flip(v), flip(logf_b)))
    o = _rmsnorm(o_fwd + o_bwd, out_norm_w)
    gate = jax.nn.silu(hg.astype(jnp.float32)).reshape(B, S, HGRN_HEADS, HGRN_VAL_DIM)
    hgrn_out = (o * gate).reshape(B, S, HGRN_VAL_WIDTH).astype(xn.dtype)

    return jnp.concatenate([attn_out, hgrn_out], axis=-1) @ w_out


def _conv_ffn(xn, w_gate, w_up, conv_w, conv_b, w_down):
    a = xn @ w_gate
    a = lax.conv_general_dilated(
        a, conv_w[:, None, :], window_strides=(1,),
        padding=((CONV_WIDTH // 2, CONV_WIDTH // 2),),
        dimension_numbers=('NWC', 'WIO', 'NWC'),
        feature_group_count=D_FF) + conv_b
    b = xn @ w_up
    return (jax.nn.gelu(a, approximate=True) * b) @ w_down


def _encode(x, norm_mix_pre, w_in, hgrn_lb_fwd, hgrn_lb_bwd, hgrn_out_norm, w_out, norm_mix_post,
            norm_ffn_pre, w_gate, w_up, conv_w, conv_b, w_down, norm_ffn_post):
    for l in range(DEPTH):
        mix = _mixer(_rmsnorm(x, norm_mix_pre[l]), w_in[l], hgrn_lb_fwd, hgrn_lb_bwd,
                     hgrn_out_norm[l], w_out[l], l)
        x = x + _rmsnorm(mix, norm_mix_post[l])
        ffn = _conv_ffn(_rmsnorm(x, norm_ffn_pre[l]), w_gate[l], w_up[l], conv_w[l], conv_b[l], w_down[l])
        x = x + _rmsnorm(ffn, norm_ffn_post[l])
    return x


def setup_inputs(seed: int = 0) -> dict:
    key = jax.random.key(seed)
    ks = jax.random.split(key, 16)
    nrm = lambda k, shape, scale: jax.random.normal(k, shape, jnp.float32) * scale
    gain = lambda k, shape: 1.0 + 0.05 * jax.random.normal(k, shape, jnp.float32)
    return {
        "x_prompt": nrm(ks[0], (BATCH, SEQ, D_MODEL), 1.0),
        "x_sample": nrm(ks[1], (DEC_BATCH, DEC_SEQ, D_MODEL), 1.0),
        "norm_mix_pre": gain(ks[2], (DEPTH, D_MODEL)),
        "w_in": nrm(ks[3], (DEPTH, D_MODEL, IN_WIDTH), D_MODEL ** -0.5),
        "hgrn_lb_fwd": nrm(ks[4], (DEPTH + 1, HGRN_KEY_WIDTH), 0.1),
        "hgrn_lb_bwd": nrm(ks[5], (DEPTH + 1, HGRN_KEY_WIDTH), 0.1),
        "hgrn_out_norm": gain(ks[6], (DEPTH, HGRN_VAL_DIM)),
        "w_out": nrm(ks[7], (DEPTH, MIX_WIDTH, D_MODEL), MIX_WIDTH ** -0.5),
        "norm_mix_post": gain(ks[8], (DEPTH, D_MODEL)),
        "norm_ffn_pre": gain(ks[9], (DEPTH, D_MODEL)),
        "w_gate": nrm(ks[10], (DEPTH, D_MODEL, D_FF), D_MODEL ** -0.5),
        "w_up": nrm(ks[11], (DEPTH, D_MODEL, D_FF), D_MODEL ** -0.5),
        "conv_w": nrm(ks[12], (DEPTH, CONV_WIDTH, D_FF), CONV_WIDTH ** -0.5),
        "conv_b": nrm(ks[13], (DEPTH, D_FF), 0.01),
        "w_down": nrm(ks[14], (DEPTH, D_FF, D_MODEL), D_FF ** -0.5),
        "norm_ffn_post": gain(ks[15], (DEPTH, D_MODEL)),
    }


def reference(x_prompt, x_sample, norm_mix_pre, w_in, hgrn_lb_fwd, hgrn_lb_bwd, hgrn_out_norm, w_out,
              norm_mix_post, norm_ffn_pre, w_gate, w_up, conv_w, conv_b, w_down, norm_ffn_post):
    y_prompt = _encode(x_prompt, norm_mix_pre, w_in, hgrn_lb_fwd, hgrn_lb_bwd, hgrn_out_norm, w_out,
                       norm_mix_post, norm_ffn_pre, w_gate, w_up, conv_w, conv_b, w_down, norm_ffn_post)
    y_sample = _encode(x_sample, norm_mix_pre, w_in, hgrn_lb_fwd, hgrn_lb_bwd, hgrn_out_norm, w_out,
                       norm_mix_post, norm_ffn_pre, w_gate, w_up, conv_w, conv_b, w_down, norm_ffn_post)
    return (y_prompt, y_sample)
```

```python
import functools
import math

import jax
import jax.numpy as jnp
from jax import lax
from jax.experimental import pallas as pl
from jax.experimental.pallas import tpu as pltpu

F32 = jnp.float32
BF16 = jnp.bfloat16

D_MODEL = 1024
HEAD_DIM = 64
ATTN_HEADS = 8
HGRN_HEADS = 8
ATTN_WIDTH = ATTN_HEADS * HEAD_DIM
HGRN_WIDTH = HGRN_HEADS * HEAD_DIM
QKV_WIDTH = 3 * ATTN_WIDTH
HPROJ_WIDTH = 5 * HGRN_WIDTH
IN_WIDTH = QKV_WIDTH + HPROJ_WIDTH
DILATED_BRANCHES = ((128, 1), (512, 4), (2048, 16))
ROPE_THETA = 500000.0
ROT_DIM = HEAD_DIM // 4
ROT_HALF = ROT_DIM // 2
HGRN_CHUNK = 64
D_FF = 2816
NORM_EPS = 1e-6
NEG_INF = -1e30

LANES = 128
HEAD_PAIRS = ATTN_WIDTH // LANES
VMEM_LIMIT_BYTES = 56 * 1024 * 1024
F32_EXP_SAFE = 80.0


def _cparams(sem):
    return pltpu.CompilerParams(dimension_semantics=sem, vmem_limit_bytes=VMEM_LIMIT_BYTES)


def _rms_scale(x):
    return lax.rsqrt(jnp.mean(x * x, axis=-1, keepdims=True) + NORM_EPS)


def _sigmoid(z):
    return 1.0 / (1.0 + jnp.exp(-z))


IN_CHUNK = 512


def _inproj_body(x_ref, nw_ref, w_ref, rot_ref, qkv_ref, hp_ref):
    x = x_ref[...]
    xn = (x * _rms_scale(x) * nw_ref[...]).astype(BF16)
    cosv, sin_lo, sin_hi = rot_ref[0], rot_ref[1], rot_ref[2]
    for c in range(IN_WIDTH // IN_CHUNK):
        acc = jnp.dot(xn, w_ref[:, c * IN_CHUNK:(c + 1) * IN_CHUNK], preferred_element_type=F32)
        lo = c * IN_CHUNK
        if lo < 2 * ATTN_WIDTH:
            for g in range(IN_CHUNK // LANES):
                a = acc[:, g * LANES:(g + 1) * LANES]
                r = (a * cosv + pltpu.roll(a, LANES - ROT_HALF, 1) * sin_lo
                     + pltpu.roll(a, ROT_HALF, 1) * sin_hi)
                if lo < ATTN_WIDTH:
                    r = r * (1.0 / math.sqrt(HEAD_DIM))
                qkv_ref[:, lo + g * LANES:lo + (g + 1) * LANES] = r.astype(BF16)
        elif lo < QKV_WIDTH:
            qkv_ref[:, lo:lo + IN_CHUNK] = acc.astype(BF16)
        else:
            hp_ref[:, lo - QKV_WIDTH:lo - QKV_WIDTH + IN_CHUNK] = acc


def _rotary_tables(seq):
    inv_freq = ROPE_THETA ** (-jnp.arange(ROT_HALF, dtype=F32) * 2.0 / ROT_DIM)
    ang = jnp.arange(seq).astype(F32)[:, None] * inv_freq[None, :]
    cos, sin = jnp.cos(ang), jnp.sin(ang)
    zeros = jnp.zeros((seq, HEAD_DIM - ROT_DIM), F32)
    half0 = jnp.zeros((seq, ROT_HALF), F32)
    cos_h = jnp.concatenate([cos, cos, 1.0 + zeros], axis=1)
    lo_h = jnp.concatenate([-sin, half0, zeros], axis=1)
    hi_h = jnp.concatenate([half0, sin, zeros], axis=1)
    per_head = jnp.stack([cos_h, lo_h, hi_h], axis=0)
    return jnp.concatenate([per_head] * (LANES // HEAD_DIM), axis=2)


def _inproj(x2d, norm_w, w_in_bf16, rot, seq, tm):
    tokens = x2d.shape[0]
    tiles_per_seq = seq // tm
    return pl.pallas_call(
        _inproj_body,
        grid=(tokens // tm,),
        in_specs=[
            pl.BlockSpec((tm, D_MODEL), lambda i: (i, 0)),
            pl.BlockSpec((1, D_MODEL), lambda i: (0, 0)),
            pl.BlockSpec((D_MODEL, IN_WIDTH), lambda i: (0, 0), pipeline_mode=pl.Buffered(1)),
            pl.BlockSpec((3, tm, LANES), lambda i: (0, i % tiles_per_seq, 0)),
        ],
        out_specs=[
            pl.BlockSpec((tm, QKV_WIDTH), lambda i: (i, 0)),
            pl.BlockSpec((tm, HPROJ_WIDTH), lambda i: (i, 0)),
        ],
        out_shape=[
            jax.ShapeDtypeStruct((tokens, QKV_WIDTH), BF16),
            jax.ShapeDtypeStruct((tokens, HPROJ_WIDTH), F32),
        ],
        compiler_params=_cparams(("arbitrary",)),
        name="inproj",
    )(x2d, norm_w, w_in_bf16, rot)


def _attn_body(qkv_ref, o_ref, lse_ref, *, sub_len, tq, ks_len, n_side):
    lane = lax.broadcasted_iota(jnp.int32, (1, LANES), 1)
    head0 = lane < HEAD_DIM
    q_iota = lax.broadcasted_iota(jnp.int32, (tq, 1), 0)
    k_iota = lax.broadcasted_iota(jnp.int32, (1, ks_len), 1)

    def q_block(i, carry):
        q0 = pl.multiple_of(i * tq, tq)
        ks = pl.multiple_of(jnp.clip(q0 - n_side, 0, sub_len - ks_len), n_side)
        band = jnp.abs((q0 + q_iota) - (ks + k_iota)) <= n_side
        for hp in range(HEAD_PAIRS):
            c0 = hp * LANES
            q = qkv_ref[pl.ds(q0, tq), c0:c0 + LANES]
            k = qkv_ref[pl.ds(ks, ks_len), ATTN_WIDTH + c0:ATTN_WIDTH + c0 + LANES]
            v = qkv_ref[pl.ds(ks, ks_len), 2 * ATTN_WIDTH + c0:2 * ATTN_WIDTH + c0 + LANES]
            outs, lses = [], []
            for h in range(LANES // HEAD_DIM):
                hm = head0 if h == 0 else jnp.logical_not(head0)
                qh = jnp.where(hm, q, jnp.zeros_like(q))
                s = lax.dot_general(qh, k, (((1,), (1,)), ((), ())), preferred_element_type=F32)
                s = jnp.where(band, s, NEG_INF)
                m = jnp.max(s, axis=-1, keepdims=True)
                p = jnp.exp(s - m)
                l = jnp.sum(p, axis=-1, keepdims=True)
                pv = jnp.dot(p.astype(BF16), v, preferred_element_type=F32)
                outs.append(pv * (1.0 / l))
                lses.append(m + jnp.log(l))
            o_ref[pl.ds(q0, tq), c0:c0 + LANES] = jnp.where(head0, outs[0], outs[1]).astype(BF16)
            lse_ref[pl.ds(q0, tq), c0:c0 + LANES] = jnp.where(head0, lses[0], lses[1])
        return carry

    lax.fori_loop(0, sub_len // tq, q_block, 0)


def _attn_branch(qkv, batch, seq, window, dilation):
    sub_len = seq // dilation
    n_side = window // (2 * dilation)
    tq = min(128, sub_len)
    ks_len = min(sub_len, tq + 2 * n_side)
    qkv_v = qkv.reshape(batch, sub_len, dilation * QKV_WIDTH)
    body = functools.partial(_attn_body, sub_len=sub_len, tq=tq, ks_len=ks_len, n_side=n_side)
    o, lse = pl.pallas_call(
        body,
        grid=(batch, dilation),
        in_specs=[pl.BlockSpec((None, sub_len, QKV_WIDTH), lambda b, r: (b, 0, r))],
        out_specs=[
            pl.BlockSpec((None, sub_len, ATTN_WIDTH), lambda b, r: (b, 0, r)),
            pl.BlockSpec((None, sub_len, ATTN_WIDTH), lambda b, r: (b, 0, r)),
        ],
        out_shape=[
            jax.ShapeDtypeStruct((batch, sub_len, dilation * ATTN_WIDTH), BF16),
            jax.ShapeDtypeStruct((batch, sub_len, dilation * ATTN_WIDTH), F32),
        ],
        compiler_params=_cparams(("arbitrary", "arbitrary")),
        name=f"attn_d{dilation}",
    )(qkv_v)
    return o.reshape(batch * seq, ATTN_WIDTH), lse.reshape(batch * seq, ATTN_WIDTH)


HGRN_UNROLL = 2


def _head_block_ones():
    r = lax.broadcasted_iota(jnp.int32, (LANES, LANES), 0) // HEAD_DIM
    c = lax.broadcasted_iota(jnp.int32, (LANES, LANES), 1) // HEAD_DIM
    return r == c


def _hgrn_direction(zq_ref, zf_ref, v_ref, lb, oacc_ref, *, seq, reverse, fast):
    cs = HGRN_CHUNK
    n_chunks = seq // cs
    lane = lax.broadcasted_iota(jnp.int32, (1, LANES), 1)
    head0 = lane < HEAD_DIM
    same_head = _head_block_ones()
    t_i = lax.broadcasted_iota(jnp.int32, (cs, cs), 0)
    s_i = lax.broadcasted_iota(jnp.int32, (cs, cs), 1)
    t_l = lax.broadcasted_iota(jnp.int32, (cs, LANES), 0)
    s_l = lax.broadcasted_iota(jnp.int32, (cs, LANES), 1) % HEAD_DIM
    if reverse:
        cum = (s_i >= t_i).astype(F32)
        keep = s_l >= t_l
        last = 0
    else:
        cum = (s_i <= t_i).astype(F32)
        keep = s_l <= t_l
        last = cs - 1
    one_m_lb = 1.0 - lb

    def split_heads(x):
        zero = jnp.zeros_like(x)
        return jnp.concatenate([jnp.where(head0, x, zero), jnp.where(head0, zero, x)], axis=0)

    def chunk(c, st):
        r0 = pl.multiple_of(c * cs, cs)
        zf = zf_ref[pl.ds(r0, cs), :]
        zq = zq_ref[pl.ds(r0, cs), :]
        v = v_ref[pl.ds(r0, cs), :]
        g = jnp.log(lb + one_m_lb * _sigmoid(zf))
        kk = one_m_lb * _sigmoid(-zf)
        q = zq * _sigmoid(zq)
        a = jnp.dot(cum, g, preferred_element_type=F32, precision=lax.Precision.HIGHEST)
        a_end = a[last:last + 1, :]
        qt = q * jnp.exp(a)
        qt_b = qt.astype(BF16)
        o = lax.dot_general(qt_b, st.astype(BF16), (((1,), (1,)), ((), ())),
                            preferred_element_type=F32)
        if fast:
            kt = split_heads(kk * jnp.exp(-a)).astype(BF16)
            att = lax.dot_general(qt_b, kt, (((1,), (1,)), ((), ())), preferred_element_type=F32)
        else:
            col = lax.broadcasted_iota(jnp.int32, (LANES, LANES), 1)

            def column(s, att):
                sel = lax.broadcasted_iota(jnp.int32, (cs, 1), 0) == s
                a_s = jnp.sum(jnp.where(sel, a, 0.0), axis=0, keepdims=True)
                kk_s = jnp.sum(jnp.where(sel, kk, 0.0), axis=0, keepdims=True)
                e = jnp.exp(jnp.minimum(a - a_s, 0.0)) * (q * kk_s)
                place = jnp.logical_and(same_head, col % HEAD_DIM == s).astype(F32)
                return att + jnp.dot(e, place, preferred_element_type=F32,
                                     precision=lax.Precision.HIGHEST)

            att = lax.fori_loop(0, cs, column, jnp.zeros((cs, LANES), F32))
        att = jnp.where(keep, att, 0.0).astype(BF16)
        o = o + jnp.dot(att, split_heads(v).astype(BF16), preferred_element_type=F32)
        k_end = (kk * jnp.exp(a_end - a)).astype(BF16)
        upd = lax.dot_general(v.astype(BF16), k_end, (((0,), (0,)), ((), ())),
                              preferred_element_type=F32)
        st = jnp.where(same_head, st * jnp.exp(a_end) + upd, 0.0)
        if reverse:
            oacc_ref[pl.ds(r0, cs), :] += o
        else:
            oacc_ref[pl.ds(r0, cs), :] = o
        return st

    def step(i, st):
        for u in range(HGRN_UNROLL):
            j = i * HGRN_UNROLL + u
            st = chunk(n_chunks - 1 - j if reverse else j, st)
        return st

    lax.fori_loop(0, n_chunks // HGRN_UNROLL, step, jnp.zeros((LANES, LANES), F32))


def _lower_bound(lbp_ref):
    p = lbp_ref[...]
    e = jnp.exp(p - jnp.max(p, axis=0, keepdims=True))
    return e[0:1, :] / jnp.sum(e, axis=0, keepdims=True)


def _hgrn_body(zq_ref, zff_ref, zfb_ref, v_ref, zg_ref, lbf_ref, lbb_ref, nw_ref, out_ref,
               oacc_ref, *, seq):
    lb_f = _lower_bound(lbf_ref)
    lb_b = _lower_bound(lbb_ref)
    for zf_ref, lb, reverse in ((zff_ref, lb_f, False), (zfb_ref, lb_b, True)):
        worst = jnp.max(-jnp.log(lb)) * HGRN_CHUNK
        run = functools.partial(_hgrn_direction, zq_ref, zf_ref, v_ref, lb, oacc_ref,
                                seq=seq, reverse=reverse)

        @pl.when(worst <= F32_EXP_SAFE)
        def _():
            run(fast=True)

        @pl.when(jnp.logical_not(worst <= F32_EXP_SAFE))
        def _():
            run(fast=False)

    ones_blk = _head_block_ones().astype(BF16)
    rows = 512 if seq % 512 == 0 else HGRN_CHUNK

    def finish(i, carry):
        r0 = pl.multiple_of(i * rows, rows)
        o = oacc_ref[pl.ds(r0, rows), :]
        sq = o * o
        hi = sq.astype(BF16)
        lo = (sq - hi.astype(F32)).astype(BF16)
        ms = (jnp.dot(hi, ones_blk, preferred_element_type=F32)
              + jnp.dot(lo, ones_blk, preferred_element_type=F32)) * (1.0 / HEAD_DIM)
        zg = zg_ref[pl.ds(r0, rows), :]
        y = o * lax.rsqrt(ms + NORM_EPS) * nw_ref[...] * (zg * _sigmoid(zg))
        out_ref[pl.ds(r0, rows), :] = y.astype(BF16)
        return carry

    lax.fori_loop(0, seq // rows, finish, 0)


def _hgrn(hproj, lb_fwd, lb_bwd, out_norm_w, batch, seq):
    hp3 = hproj.reshape(batch, seq, HPROJ_WIDTH)
    nslots = lb_fwd.shape[0]
    cols = HGRN_WIDTH // LANES

    def slab(k):
        return pl.BlockSpec((None, seq, LANES), lambda b, p, k=k: (b, 0, k * cols + p))

    out = pl.pallas_call(
        functools.partial(_hgrn_body, seq=seq),
        grid=(batch, cols),
        in_specs=[slab(0), slab(1), slab(2), slab(3), slab(4),
                  pl.BlockSpec((nslots, LANES), lambda b, p: (0, p)),
                  pl.BlockSpec((nslots, LANES), lambda b, p: (0, p)),
                  pl.BlockSpec((1, LANES), lambda b, p: (0, 0))],
        out_specs=pl.BlockSpec((None, seq, LANES), lambda b, p: (b, 0, p)),
        out_shape=jax.ShapeDtypeStruct((batch, seq, HGRN_WIDTH), BF16),
        scratch_shapes=[pltpu.VMEM((seq, LANES), F32)],
        compiler_params=_cparams(("arbitrary", "arbitrary")),
        name="hgrn",
    )(hp3, hp3, hp3, hp3, hp3, lb_fwd, lb_bwd, out_norm_w)
    return out.reshape(batch * seq, HGRN_WIDTH)


def _outproj_body(o1_ref, o2_ref, o3_ref, l1_ref, l2_ref, l3_ref, hg_ref, x_ref, w_ref, nw_ref,
                  h_ref):
    l1, l2, l3 = l1_ref[...], l2_ref[...], l3_ref[...]
    m = jnp.maximum(jnp.maximum(l1, l2), l3)
    e1, e2, e3 = jnp.exp(l1 - m), jnp.exp(l2 - m), jnp.exp(l3 - m)
    num = (e1 * o1_ref[...].astype(F32) + e2 * o2_ref[...].astype(F32)
           + e3 * o3_ref[...].astype(F32))
    attn = (num * (1.0 / (e1 + e2 + e3))).astype(BF16)
    mix = (jnp.dot(attn, w_ref[0:ATTN_WIDTH, :], preferred_element_type=F32)
           + jnp.dot(hg_ref[...], w_ref[ATTN_WIDTH:, :], preferred_element_type=F32))
    h_ref[...] = x_ref[...] + mix * _rms_scale(mix) * nw_ref[...]


def _outproj(outs, lses, hg, x2d, w_out_bf16, norm_w, tm):
    tokens = x2d.shape[0]
    half = pl.BlockSpec((tm, ATTN_WIDTH), lambda i: (i, 0))
    full = pl.BlockSpec((tm, D_MODEL), lambda i: (i, 0))
    return pl.pallas_call(
        _outproj_body,
        grid=(tokens // tm,),
        in_specs=[half] * 7 + [
            full,
            pl.BlockSpec((D_MODEL, D_MODEL), lambda i: (0, 0), pipeline_mode=pl.Buffered(1)),
            pl.BlockSpec((1, D_MODEL), lambda i: (0, 0)),
        ],
        out_specs=full,
        out_shape=jax.ShapeDtypeStruct((tokens, D_MODEL), F32),
        compiler_params=_cparams(("arbitrary",)),
        name="outproj",
    )(*outs, *lses, hg, x2d, w_out_bf16, norm_w)


FF_CHUNK = 256
HALO = 8


def _ffn_body(prev_ref, h_ref, next_ref, nw_pre_ref, wg_ref, wu_ref, cw_ref, cb_ref, wd_ref,
              nw_post_ref, y_ref, g_ref, *, tm, tiles_per_seq):
    i = pl.program_id(0)
    has_prev = (i % tiles_per_seq != 0).astype(F32)
    has_next = (i % tiles_per_seq != tiles_per_seq - 1).astype(F32)
    h = h_ref[...]
    nw = nw_pre_ref[...]

    def normed(t):
        return t * _rms_scale(t) * nw

    xn_main = normed(h).astype(BF16)
    xn_ext = jnp.concatenate([normed(prev_ref[...]) * has_prev, normed(h),
                              normed(next_ref[...]) * has_next], axis=0).astype(BF16)
    ext = tm + 2 * HALO
    for c in range(D_FF // FF_CHUNK):
        cols = slice(c * FF_CHUNK, (c + 1) * FF_CHUNK)
        a = jnp.dot(xn_ext, wg_ref[:, cols], preferred_element_type=F32)
        a_prev = pltpu.roll(a, 1, 0)[HALO:HALO + tm]
        a_next = pltpu.roll(a, ext - 1, 0)[HALO:HALO + tm]
        a_mid = a[HALO:HALO + tm]
        conv = (a_prev * cw_ref[0:1, cols] + a_mid * cw_ref[1:2, cols]
                + a_next * cw_ref[2:3, cols] + cb_ref[:, cols])
        b = jnp.dot(xn_main, wu_ref[:, cols], preferred_element_type=F32)
        gelu = 0.5 * conv * (1.0 + jnp.tanh(math.sqrt(2.0 / math.pi)
                                            * (conv + 0.044715 * (conv * conv * conv))))
        g_ref[:, cols] = (gelu * b).astype(BF16)
    ffn = jnp.dot(g_ref[...], wd_ref[...], preferred_element_type=F32)
    y_ref[...] = h + ffn * _rms_scale(ffn) * nw_post_ref[...]


def _ffn(h2d, nw_pre, wg, wu, conv_w, conv_b, wd, nw_post, seq, tm):
    tokens = h2d.shape[0]
    tiles_per_seq = seq // tm
    hb = tm // HALO
    n_halo = tokens // HALO
    resident = dict(pipeline_mode=pl.Buffered(1))
    return pl.pallas_call(
        functools.partial(_ffn_body, tm=tm, tiles_per_seq=tiles_per_seq),
        grid=(tokens // tm,),
        in_specs=[
            pl.BlockSpec((HALO, D_MODEL), lambda i: (jnp.maximum(i * hb - 1, 0), 0)),
            pl.BlockSpec((tm, D_MODEL), lambda i: (i, 0)),
            pl.BlockSpec((HALO, D_MODEL), lambda i: (jnp.minimum((i + 1) * hb, n_halo - 1), 0)),
            pl.BlockSpec((1, D_MODEL), lambda i: (0, 0)),
            pl.BlockSpec((D_MODEL, D_FF), lambda i: (0, 0), **resident),
            pl.BlockSpec((D_MODEL, D_FF), lambda i: (0, 0), **resident),
            pl.BlockSpec((3, D_FF), lambda i: (0, 0)),
            pl.BlockSpec((1, D_FF), lambda i: (0, 0)),
            pl.BlockSpec((D_FF, D_MODEL), lambda i: (0, 0), **resident),
            pl.BlockSpec((1, D_MODEL), lambda i: (0, 0)),
        ],
        out_specs=pl.BlockSpec((tm, D_MODEL), lambda i: (i, 0)),
        out_shape=jax.ShapeDtypeStruct((tokens, D_MODEL), F32),
        scratch_shapes=[pltpu.VMEM((tm, D_FF), BF16)],
        compiler_params=_cparams(("arbitrary",)),
        name="ffn",
    )(h2d, h2d, h2d, nw_pre, wg, wu, conv_w, conv_b, wd, nw_post)


def _token_tile(seq):
    return 512 if seq % 512 == 0 else seq


def _encode(x, p):
    batch, seq, _ = x.shape
    tm = _token_tile(seq)
    x2d = x.reshape(batch * seq, D_MODEL)
    qkv, hproj = _inproj(x2d, p["norm_mix_pre"], p["w_in"], _rotary_tables(seq), seq, tm)
    outs, lses = [], []
    for window, dilation in DILATED_BRANCHES:
        o, lse = _attn_branch(qkv, batch, seq, window, dilation)
        outs.append(o)
        lses.append(lse)
    hg = _hgrn(hproj, p["lb_fwd"], p["lb_bwd"], p["hgrn_out_norm"], batch, seq)
    h = _outproj(outs, lses, hg, x2d, p["w_out"], p["norm_mix_post"], tm)
    y = _ffn(h, p["norm_ffn_pre"], p["w_gate"], p["w_up"], p["conv_w"], p["conv_b"], p["w_down"],
             p["norm_ffn_post"], seq, tm)
    return y.reshape(batch, seq, D_MODEL)


def kernel(x_prompt, x_sample, norm_mix_pre, w_in, hgrn_lb_fwd, hgrn_lb_bwd, hgrn_out_norm, w_out,
           norm_mix_post, norm_ffn_pre, w_gate, w_up, conv_w, conv_b, w_down, norm_ffn_post):
    assert w_in.shape[0] == 1, "one layer"
    p = {
        "norm_mix_pre": norm_mix_pre[0][None, :],
        "w_in": w_in[0].astype(BF16),
        "lb_fwd": hgrn_lb_fwd.astype(F32),
        "lb_bwd": hgrn_lb_bwd.astype(F32),
        "hgrn_out_norm": jnp.tile(hgrn_out_norm[0], LANES // HEAD_DIM)[None, :],
        "w_out": w_out[0].astype(BF16),
        "norm_mix_post": norm_mix_post[0][None, :],
        "norm_ffn_pre": norm_ffn_pre[0][None, :],
        "w_gate": w_gate[0].astype(BF16),
        "w_up": w_up[0].astype(BF16),
        "conv_w": conv_w[0],
        "conv_b": conv_b[0][None, :],
        "w_down": w_down[0].astype(BF16),
        "norm_ffn_post": norm_ffn_post[0][None, :],
    }
    return (_encode(x_prompt, p), _encode(x_sample, p))
```

```python
import functools
import math

import jax
import jax.numpy as jnp
from jax import lax
from jax.experimental import pallas as pl
from jax.experimental.pallas import tpu as pltpu

F32 = jnp.float32
BF16 = jnp.bfloat16

D_MODEL = 1024
HEAD_DIM = 64
ATTN_HEADS = 8
HGRN_HEADS = 8
ATTN_WIDTH = ATTN_HEADS * HEAD_DIM
HGRN_WIDTH = HGRN_HEADS * HEAD_DIM
QKV_WIDTH = 3 * ATTN_WIDTH
HPROJ_WIDTH = 5 * HGRN_WIDTH
IN_WIDTH = QKV_WIDTH + HPROJ_WIDTH
DILATED_BRANCHES = ((128, 1), (512, 4), (2048, 16))
ROPE_THETA = 500000.0
ROT_DIM = HEAD_DIM // 4
ROT_HALF = ROT_DIM // 2
HGRN_CHUNK = 64
D_FF = 2816
NORM_EPS = 1e-6
NEG_INF = -1e30

LANES = 128
SUBLANES = 8
HEAD_PAIRS = ATTN_WIDTH // LANES
VMEM_LIMIT_BYTES = 56 * 1024 * 1024
F32_EXP_SAFE = 80.0


def _cparams(sem):
    return pltpu.CompilerParams(dimension_semantics=sem, vmem_limit_bytes=VMEM_LIMIT_BYTES)


def _rms_scale(x):
    return lax.rsqrt(jnp.mean(x * x, axis=-1, keepdims=True) + NORM_EPS)


def _sigmoid(z):
    return 1.0 / (1.0 + jnp.exp(-z))


def _head_block_ones():
    r = lax.broadcasted_iota(jnp.int32, (LANES, LANES), 0) // HEAD_DIM
    c = lax.broadcasted_iota(jnp.int32, (LANES, LANES), 1) // HEAD_DIM
    return r == c


IN_CHUNK = 512


def _inproj_body(x_ref, nw_ref, w_ref, rot_ref, qkv_ref, hp_ref):
    x = x_ref[...]
    xn = (x * _rms_scale(x) * nw_ref[...]).astype(BF16)
    cosv, sin_lo, sin_hi = rot_ref[0], rot_ref[1], rot_ref[2]
    for c in range(IN_WIDTH // IN_CHUNK):
        acc = jnp.dot(xn, w_ref[:, c * IN_CHUNK:(c + 1) * IN_CHUNK], preferred_element_type=F32)
        lo = c * IN_CHUNK
        if lo < 2 * ATTN_WIDTH:
            for g in range(IN_CHUNK // LANES):
                a = acc[:, g * LANES:(g + 1) * LANES]
                r = (a * cosv + pltpu.roll(a, LANES - ROT_HALF, 1) * sin_lo
                     + pltpu.roll(a, ROT_HALF, 1) * sin_hi)
                if lo < ATTN_WIDTH:
                    r = r * (1.0 / math.sqrt(HEAD_DIM))
                qkv_ref[:, lo + g * LANES:lo + (g + 1) * LANES] = r.astype(BF16)
        elif lo < QKV_WIDTH:
            qkv_ref[:, lo:lo + IN_CHUNK] = acc.astype(BF16)
        else:
            hp_ref[:, lo - QKV_WIDTH:lo - QKV_WIDTH + IN_CHUNK] = acc


def _rotary_tables(seq):
    inv_freq = ROPE_THETA ** (-jnp.arange(ROT_HALF, dtype=F32) * 2.0 / ROT_DIM)
    ang = jnp.arange(seq).astype(F32)[:, None] * inv_freq[None, :]
    cos, sin = jnp.cos(ang), jnp.sin(ang)
    zeros = jnp.zeros((seq, HEAD_DIM - ROT_DIM), F32)
    half0 = jnp.zeros((seq, ROT_HALF), F32)
    cos_h = jnp.concatenate([cos, cos, 1.0 + zeros], axis=1)
    lo_h = jnp.concatenate([-sin, half0, zeros], axis=1)
    hi_h = jnp.concatenate([half0, sin, zeros], axis=1)
    per_head = jnp.stack([cos_h, lo_h, hi_h], axis=0)
    return jnp.concatenate([per_head] * (LANES // HEAD_DIM), axis=2)


def _inproj(x2d, norm_w, w_in_bf16, rot, seq, tm):
    tokens = x2d.shape[0]
    tiles_per_seq = seq // tm
    return pl.pallas_call(
        _inproj_body,
        grid=(tokens // tm,),
        in_specs=[
            pl.BlockSpec((tm, D_MODEL), lambda i: (i, 0)),
            pl.BlockSpec((1, D_MODEL), lambda i: (0, 0)),
            pl.BlockSpec((D_MODEL, IN_WIDTH), lambda i: (0, 0), pipeline_mode=pl.Buffered(1)),
            pl.BlockSpec((3, tm, LANES), lambda i: (0, i % tiles_per_seq, 0)),
        ],
        out_specs=[
            pl.BlockSpec((tm, QKV_WIDTH), lambda i: (i, 0)),
            pl.BlockSpec((tm, HPROJ_WIDTH), lambda i: (i, 0)),
        ],
        out_shape=[
            jax.ShapeDtypeStruct((tokens, QKV_WIDTH), BF16),
            jax.ShapeDtypeStruct((tokens, HPROJ_WIDTH), F32),
        ],
        compiler_params=_cparams(("arbitrary",)),
        name="inproj",
    )(x2d, norm_w, w_in_bf16, rot)


ATTN_TQ = 128
ATTN_GROUP = 4
MERGE_ROWS = 256


def _attn_body(q_ref, k_ref, v_ref, out_ref, stage_a, stage_b, perm_ref, o_ref, lse_ref, *, seq):
    lane = lax.broadcasted_iota(jnp.int32, (1, LANES), 1)
    head0 = lane < HEAD_DIM

    dils = [d for _, d in DILATED_BRANCHES]
    for ai, src in enumerate((q_ref, k_ref, v_ref)):
        stage_a[...] = src[...].astype(F32)
        stages = (stage_a, stage_b)
        for li in range(1, len(dils)):
            d_prev, dil = dils[li - 1], dils[li]
            ratio = dil // d_prev
            len_prev, sub_len = seq // d_prev, seq // dil
            s_in, s_out = stages[(li - 1) % 2], stages[li % 2]
            dst = perm_ref.at[li - 1, ai]
            keep_f32 = li + 1 < len(dils)

            def gather(t, carry, s_in=s_in, s_out=s_out, dst=dst, d_prev=d_prev, ratio=ratio,
                       len_prev=len_prev, sub_len=sub_len, keep_f32=keep_f32):
                seg, sub = t // ratio, t % ratio
                rows = s_in[pl.ds(seg * len_prev + sub, sub_len, stride=ratio), :]
                out_rows = pl.ds(pl.multiple_of((seg + d_prev * sub) * sub_len, sub_len), sub_len)
                dst[out_rows, :] = rows.astype(BF16)
                if keep_f32:
                    s_out[out_rows, :] = rows
                return carry

            lax.fori_loop(0, dil, gather, 0)

    for bi, (window, dil) in enumerate(DILATED_BRANCHES):
        sub_len = seq // dil
        n_side = window // (2 * dil)
        tq = min(ATTN_TQ, sub_len)
        ks_len = min(sub_len, tq + 2 * n_side)
        nqb = sub_len // tq
        if dil == 1:
            srcs = (q_ref, k_ref, v_ref)
        else:
            srcs = tuple(perm_ref.at[bi - 1, ai] for ai in range(3))

        q_iota = lax.broadcasted_iota(jnp.int32, (tq, 1), 0)
        k_iota = lax.broadcasted_iota(jnp.int32, (1, ks_len), 1)

        def q_group(it, carry, bi=bi, dil=dil, sub_len=sub_len, n_side=n_side, tq=tq,
                    ks_len=ks_len, nqb=nqb, srcs=srcs, q_iota=q_iota, k_iota=k_iota):
            work = []
            for n in range(ATTN_GROUP):
                f = it * ATTN_GROUP + n
                r = f // nqb
                q0 = (f % nqb) * tq
                base = r * sub_len
                ks = jnp.clip(q0 - n_side, 0, sub_len - ks_len)
                band = jnp.abs((q0 + q_iota) - (ks + k_iota)) <= n_side
                q = srcs[0][pl.ds(pl.multiple_of(base + q0, tq), tq), :]
                k = srcs[1][pl.ds(pl.multiple_of(base + ks, n_side), ks_len), :]
                v = srcs[2][pl.ds(pl.multiple_of(base + ks, n_side), ks_len), :]
                scores = []
                for h in range(LANES // HEAD_DIM):
                    hm = head0 if h == 0 else jnp.logical_not(head0)
                    qh = jnp.where(hm, q, jnp.zeros_like(q))
                    scores.append(lax.dot_general(qh, k, (((1,), (1,)), ((), ())),
                                                  preferred_element_type=F32))
                if dil == 1:
                    rows = pl.ds(pl.multiple_of(q0, tq), tq)
                else:
                    rows = pl.ds(r + q0 * dil, tq, stride=dil)
                work.append((scores, band, v, rows))
            for scores, band, v, rows in work:
                outs, lses = [], []
                for s in scores:
                    s = jnp.where(band, s, NEG_INF)
                    m = jnp.max(s, axis=-1, keepdims=True)
                    p = jnp.exp(s - m)
                    l = jnp.sum(p, axis=-1, keepdims=True)
                    pv = jnp.dot(p.astype(BF16), v, preferred_element_type=F32)
                    outs.append(pv * (1.0 / l))
                    lses.append(m + jnp.log(l))
                o_ref.at[bi][rows, :] = jnp.where(head0, outs[0], outs[1])
                lse_ref.at[bi][rows, :] = jnp.where(head0, lses[0], lses[1])
            return carry

        lax.fori_loop(0, dil * nqb // ATTN_GROUP, q_group, 0)

    rows_m = min(MERGE_ROWS, seq)

    def merge(i, carry):
        rows = pl.ds(pl.multiple_of(i * rows_m, rows_m), rows_m)
        l1, l2, l3 = lse_ref.at[0][rows, :], lse_ref.at[1][rows, :], lse_ref.at[2][rows, :]
        m = jnp.maximum(jnp.maximum(l1, l2), l3)
        e1, e2, e3 = jnp.exp(l1 - m), jnp.exp(l2 - m), jnp.exp(l3 - m)
        num = e1 * o_ref.at[0][rows, :] + e2 * o_ref.at[1][rows, :] + e3 * o_ref.at[2][rows, :]
        out_ref[rows, :] = (num * (1.0 / (e1 + e2 + e3))).astype(BF16)
        return carry

    lax.fori_loop(0, seq // rows_m, merge, 0)


def _attention(qkv, batch, seq):
    qkv3 = qkv.reshape(batch, seq, QKV_WIDTH)
    nb = len(DILATED_BRANCHES)
    dils = [d for _, d in DILATED_BRANCHES]
    assert dils[0] == 1 and all(b % a == 0 for a, b in zip(dils, dils[1:]))

    def slab(k):
        return pl.BlockSpec((None, seq, LANES), lambda b, p, k=k: (b, 0, k * HEAD_PAIRS + p))

    out = pl.pallas_call(
        functools.partial(_attn_body, seq=seq),
        grid=(batch, HEAD_PAIRS),
        in_specs=[slab(0), slab(1), slab(2)],
        out_specs=pl.BlockSpec((None, seq, LANES), lambda b, p: (b, 0, p)),
        out_shape=jax.ShapeDtypeStruct((batch, seq, ATTN_WIDTH), BF16),
        scratch_shapes=[pltpu.VMEM((seq, LANES), F32),
                        pltpu.VMEM((seq, LANES), F32),
                        pltpu.VMEM((nb - 1, 3, seq, LANES), BF16),
                        pltpu.VMEM((nb, seq, LANES), F32),
                        pltpu.VMEM((nb, seq, LANES), F32)],
        compiler_params=_cparams(("arbitrary", "arbitrary")),
        name="attention",
    )(qkv3, qkv3, qkv3)
    return out.reshape(batch * seq, ATTN_WIDTH)


HGRN_UNROLL = 8
HGRN_PREP_ROWS = 512


def _chunk_cumsum(g, reverse):
    rows = g.shape[0]
    pos = lax.broadcasted_iota(jnp.int32, (rows, 1), 0) % HGRN_CHUNK
    sh = 1
    while sh < HGRN_CHUNK:
        if reverse:
            shifted = pltpu.roll(g, rows - sh, 0)
            valid = pos < HGRN_CHUNK - sh
        else:
            shifted = pltpu.roll(g, sh, 0)
            valid = pos >= sh
        g = g + jnp.where(valid, shifted, 0.0)
        sh *= 2
    return g


def _gates(zf, zq, lb):
    one_m_lb = 1.0 - lb
    g = jnp.log(lb + one_m_lb * _sigmoid(zf))
    kk = one_m_lb * _sigmoid(-zf)
    q = zq * _sigmoid(zq)
    return g, kk, q


def _split_heads(x, head0):
    zero = jnp.zeros_like(x)
    return jnp.concatenate([jnp.where(head0, x, zero), jnp.where(head0, zero, x)], axis=0)


def _causal_keep(reverse):
    t_l = lax.broadcasted_iota(jnp.int32, (HGRN_CHUNK, LANES), 0)
    s_l = lax.broadcasted_iota(jnp.int32, (HGRN_CHUNK, LANES), 1) % HEAD_DIM
    return (s_l >= t_l) if reverse else (s_l <= t_l)


def _hgrn_fast(zq_ref, zf_ref, vb_ref, lb, oacc_ref, qt_ref, kt_ref, ke_ref, dec_ref,
               *, seq, reverse):
    cs = HGRN_CHUNK
    n_chunks = seq // cs
    prep = HGRN_PREP_ROWS if seq % HGRN_PREP_ROWS == 0 else cs * SUBLANES
    cpb = prep // cs
    head0 = lax.broadcasted_iota(jnp.int32, (1, LANES), 1) < HEAD_DIM
    same_head = _head_block_ones()
    keep = _causal_keep(reverse)

    def prepare(i, carry):
        rows = pl.ds(pl.multiple_of(i * prep, prep), prep)
        g, kk, q = _gates(zf_ref[rows, :], zq_ref[rows, :], lb)
        a = _chunk_cumsum(g, reverse).reshape(cpb, cs, LANES)
        a_end = jnp.sum(g.reshape(cpb, cs, LANES), axis=1)
        kk3 = kk.reshape(cpb, cs, LANES)
        qt_ref[rows, :] = (q.reshape(cpb, cs, LANES) * jnp.exp(a)).reshape(prep, LANES).astype(BF16)
        kt_ref[rows, :] = (kk3 * jnp.exp(-a)).reshape(prep, LANES).astype(BF16)
        ke_ref[rows, :] = (kk3 * jnp.exp(a_end[:, None, :] - a)).reshape(prep, LANES).astype(BF16)
        dec_ref[pl.ds(pl.multiple_of(i * cpb, cpb), cpb), :] = jnp.exp(a_end)
        return carry

    lax.fori_loop(0, seq // prep, prepare, 0)

    def step(i, st):
        chunks = [i * HGRN_UNROLL + u for u in range(HGRN_UNROLL)]
        if reverse:
            chunks = [n_chunks - 1 - j for j in chunks]
        rows = [pl.ds(pl.multiple_of(c * cs, cs), cs) for c in chunks]
        upds = [lax.dot_general(vb_ref[r, :], ke_ref[r, :], (((0,), (0,)), ((), ())),
                                preferred_element_type=F32) for r in rows]
        atts = [lax.dot_general(qt_ref[r, :], _split_heads(kt_ref[r, :], head0),
                                (((1,), (1,)), ((), ())), preferred_element_type=F32)
                for r in rows]
        states = []
        for c, upd in zip(chunks, upds):
            states.append(st.astype(BF16))
            st = jnp.where(same_head, st * dec_ref[pl.ds(c, 1), :] + upd, 0.0)
        for r, att, st_in in zip(rows, atts, states):
            att = jnp.where(keep, att, 0.0).astype(BF16)
            o = (jnp.dot(att, _split_heads(vb_ref[r, :], head0), preferred_element_type=F32)
                 + lax.dot_general(qt_ref[r, :], st_in, (((1,), (1,)), ((), ())),
                                   preferred_element_type=F32))
            if reverse:
                oacc_ref[r, :] += o
            else:
                oacc_ref[r, :] = o
        return st

    lax.fori_loop(0, n_chunks // HGRN_UNROLL, step, jnp.zeros((LANES, LANES), F32))


def _hgrn_safe(zq_ref, zf_ref, vb_ref, lb, oacc_ref, *, seq, reverse):
    cs = HGRN_CHUNK
    n_chunks = seq // cs
    head0 = lax.broadcasted_iota(jnp.int32, (1, LANES), 1) < HEAD_DIM
    same_head = _head_block_ones()
    keep = _causal_keep(reverse)
    col = lax.broadcasted_iota(jnp.int32, (LANES, LANES), 1)
    last = 0 if reverse else cs - 1

    def chunk(j, st):
        c = n_chunks - 1 - j if reverse else j
        rows = pl.ds(pl.multiple_of(c * cs, cs), cs)
        g, kk, q = _gates(zf_ref[rows, :], zq_ref[rows, :], lb)
        vb = vb_ref[rows, :]
        a = _chunk_cumsum(g, reverse)
        a_end = a[last:last + 1, :]
        qt = (q * jnp.exp(a)).astype(BF16)

        def column(s, att):
            sel = lax.broadcasted_iota(jnp.int32, (cs, 1), 0) == s
            a_s = jnp.sum(jnp.where(sel, a, 0.0), axis=0, keepdims=True)
            kk_s = jnp.sum(jnp.where(sel, kk, 0.0), axis=0, keepdims=True)
            e = jnp.exp(jnp.minimum(a - a_s, 0.0)) * (q * kk_s)
            place = jnp.logical_and(same_head, col % HEAD_DIM == s).astype(F32)
            return att + jnp.dot(e, place, preferred_element_type=F32,
                                 precision=lax.Precision.HIGHEST)

        att = lax.fori_loop(0, cs, column, jnp.zeros((cs, LANES), F32))
        att = jnp.where(keep, att, 0.0).astype(BF16)
        o = (jnp.dot(att, _split_heads(vb, head0), preferred_element_type=F32)
             + lax.dot_general(qt, st.astype(BF16), (((1,), (1,)), ((), ())),
                               preferred_element_type=F32))
        k_end = (kk * jnp.exp(a_end - a)).astype(BF16)
        upd = lax.dot_general(vb, k_end, (((0,), (0,)), ((), ())), preferred_element_type=F32)
        st = jnp.where(same_head, st * jnp.exp(a_end) + upd, 0.0)
        if reverse:
            oacc_ref[rows, :] += o
        else:
            oacc_ref[rows, :] = o
        return st

    lax.fori_loop(0, n_chunks, chunk, jnp.zeros((LANES, LANES), F32))


def _lower_bound(lbp_ref):
    p = lbp_ref[...]
    e = jnp.exp(p - jnp.max(p, axis=0, keepdims=True))
    return e[0:1, :] / jnp.sum(e, axis=0, keepdims=True)


def _hgrn_body(zq_ref, zff_ref, zfb_ref, v_ref, zg_ref, lbf_ref, lbb_ref, nw_ref, out_ref,
               oacc_ref, vb_ref, qt_ref, kt_ref, ke_ref, dec_ref, *, seq):
    vb_ref[...] = v_ref[...].astype(BF16)
    for zf_ref, lbp_ref, reverse in ((zff_ref, lbf_ref, False), (zfb_ref, lbb_ref, True)):
        lb = _lower_bound(lbp_ref)
        worst = jnp.max(-jnp.log(lb)) * HGRN_CHUNK

        @pl.when(worst <= F32_EXP_SAFE)
        def _():
            _hgrn_fast(zq_ref, zf_ref, vb_ref, lb, oacc_ref, qt_ref, kt_ref, ke_ref, dec_ref,
                       seq=seq, reverse=reverse)

        @pl.when(jnp.logical_not(worst <= F32_EXP_SAFE))
        def _():
            _hgrn_safe(zq_ref, zf_ref, vb_ref, lb, oacc_ref, seq=seq, reverse=reverse)

    ones_blk = _head_block_ones().astype(BF16)
    rows_n = 512 if seq % 512 == 0 else HGRN_CHUNK

    def finish(i, carry):
        rows = pl.ds(pl.multiple_of(i * rows_n, rows_n), rows_n)
        o = oacc_ref[rows, :]
        sq = o * o
        hi = sq.astype(BF16)
        lo = (sq - hi.astype(F32)).astype(BF16)
        ms = (jnp.dot(hi, ones_blk, preferred_element_type=F32)
              + jnp.dot(lo, ones_blk, preferred_element_type=F32)) * (1.0 / HEAD_DIM)
        zg = zg_ref[rows, :]
        y = o * lax.rsqrt(ms + NORM_EPS) * nw_ref[...] * (zg * _sigmoid(zg))
        out_ref[rows, :] = y.astype(BF16)
        return carry

    lax.fori_loop(0, seq // rows_n, finish, 0)


def _hgrn(hproj, lb_fwd, lb_bwd, out_norm_w, batch, seq):
    hp3 = hproj.reshape(batch, seq, HPROJ_WIDTH)
    nslots = lb_fwd.shape[0]
    cols = HGRN_WIDTH // LANES

    def slab(k):
        return pl.BlockSpec((None, seq, LANES), lambda b, p, k=k: (b, 0, k * cols + p))

    out = pl.pallas_call(
        functools.partial(_hgrn_body, seq=seq),
        grid=(batch, cols),
        in_specs=[slab(0), slab(1), slab(2), slab(3), slab(4),
                  pl.BlockSpec((nslots, LANES), lambda b, p: (0, p)),
                  pl.BlockSpec((nslots, LANES), lambda b, p: (0, p)),
                  pl.BlockSpec((1, LANES), lambda b, p: (0, 0))],
        out_specs=pl.BlockSpec((None, seq, LANES), lambda b, p: (b, 0, p)),
        out_shape=jax.ShapeDtypeStruct((batch, seq, HGRN_WIDTH), BF16),
        scratch_shapes=[pltpu.VMEM((seq, LANES), F32),
                        pltpu.VMEM((seq, LANES), BF16),
                        pltpu.VMEM((seq, LANES), BF16),
                        pltpu.VMEM((seq, LANES), BF16),
                        pltpu.VMEM((seq, LANES), BF16),
                        pltpu.VMEM((seq // HGRN_CHUNK, LANES), F32)],
        compiler_params=_cparams(("arbitrary", "arbitrary")),
        name="hgrn",
    )(hp3, hp3, hp3, hp3, hp3, lb_fwd, lb_bwd, out_norm_w)
    return out.reshape(batch * seq, HGRN_WIDTH)


def _outproj_body(attn_ref, hg_ref, x_ref, w_ref, nw_ref, h_ref):
    mix = (jnp.dot(attn_ref[...], w_ref[0:ATTN_WIDTH, :], preferred_element_type=F32)
           + jnp.dot(hg_ref[...], w_ref[ATTN_WIDTH:, :], preferred_element_type=F32))
    h_ref[...] = x_ref[...] + mix * _rms_scale(mix) * nw_ref[...]


def _outproj(attn, hg, x2d, w_out_bf16, norm_w, tm):
    tokens = x2d.shape[0]
    half = pl.BlockSpec((tm, ATTN_WIDTH), lambda i: (i, 0))
    full = pl.BlockSpec((tm, D_MODEL), lambda i: (i, 0))
    return pl.pallas_call(
        _outproj_body,
        grid=(tokens // tm,),
        in_specs=[half, half, full,
                  pl.BlockSpec((D_MODEL, D_MODEL), lambda i: (0, 0), pipeline_mode=pl.Buffered(1)),
                  pl.BlockSpec((1, D_MODEL), lambda i: (0, 0))],
        out_specs=full,
        out_shape=jax.ShapeDtypeStruct((tokens, D_MODEL), F32),
        compiler_params=_cparams(("arbitrary",)),
        name="outproj",
    )(attn, hg, x2d, w_out_bf16, norm_w)


FF_CHUNK = 256
HALO = SUBLANES


def _ffn_body(prev_ref, h_ref, next_ref, nw_pre_ref, wg_ref, wu_ref, cw_ref, cb_ref, wd_ref,
              nw_post_ref, y_ref, g_ref, *, tm, tiles_per_seq):
    i = pl.program_id(0)
    has_prev = (i % tiles_per_seq != 0).astype(F32)
    has_next = (i % tiles_per_seq != tiles_per_seq - 1).astype(F32)
    h = h_ref[...]
    nw = nw_pre_ref[...]

    def normed(t):
        return t * _rms_scale(t) * nw

    xn_main = normed(h).astype(BF16)
    xn_ext = jnp.concatenate([normed(prev_ref[...]) * has_prev, normed(h),
                              normed(next_ref[...]) * has_next], axis=0).astype(BF16)
    ext = tm + 2 * HALO
    for c in range(D_FF // FF_CHUNK):
        cols = slice(c * FF_CHUNK, (c + 1) * FF_CHUNK)
        a = jnp.dot(xn_ext, wg_ref[:, cols], preferred_element_type=F32)
        a_prev = pltpu.roll(a, 1, 0)[HALO:HALO + tm]
        a_next = pltpu.roll(a, ext - 1, 0)[HALO:HALO + tm]
        a_mid = a[HALO:HALO + tm]
        conv = (a_prev * cw_ref[0:1, cols] + a_mid * cw_ref[1:2, cols]
                + a_next * cw_ref[2:3, cols] + cb_ref[:, cols])
        b = jnp.dot(xn_main, wu_ref[:, cols], preferred_element_type=F32)
        gelu = 0.5 * conv * (1.0 + jnp.tanh(math.sqrt(2.0 / math.pi)
                                            * (conv + 0.044715 * (conv * conv * conv))))
        g_ref[:, cols] = (gelu * b).astype(BF16)
    ffn = jnp.dot(g_ref[...], wd_ref[...], preferred_element_type=F32)
    y_ref[...] = h + ffn * _rms_scale(ffn) * nw_post_ref[...]


def _ffn(h2d, nw_pre, wg, wu, conv_w, conv_b, wd, nw_post, seq, tm):
    tokens = h2d.shape[0]
    tiles_per_seq = seq // tm
    hb = tm // HALO
    n_halo = tokens // HALO
    resident = dict(pipeline_mode=pl.Buffered(1))
    return pl.pallas_call(
        functools.partial(_ffn_body, tm=tm, tiles_per_seq=tiles_per_seq),
        grid=(tokens // tm,),
        in_specs=[
            pl.BlockSpec((HALO, D_MODEL), lambda i: (jnp.maximum(i * hb - 1, 0), 0)),
            pl.BlockSpec((tm, D_MODEL), lambda i: (i, 0)),
            pl.BlockSpec((HALO, D_MODEL), lambda i: (jnp.minimum((i + 1) * hb, n_halo - 1), 0)),
            pl.BlockSpec((1, D_MODEL), lambda i: (0, 0)),
            pl.BlockSpec((D_MODEL, D_FF), lambda i: (0, 0), **resident),
            pl.BlockSpec((D_MODEL, D_FF), lambda i: (0, 0), **resident),
            pl.BlockSpec((3, D_FF), lambda i: (0, 0)),
            pl.BlockSpec((1, D_FF), lambda i: (0, 0)),
            pl.BlockSpec((D_FF, D_MODEL), lambda i: (0, 0), **resident),
            pl.BlockSpec((1, D_MODEL), lambda i: (0, 0)),
        ],
        out_specs=pl.BlockSpec((tm, D_MODEL), lambda i: (i, 0)),
        out_shape=jax.ShapeDtypeStruct((tokens, D_MODEL), F32),
        scratch_shapes=[pltpu.VMEM((tm, D_FF), BF16)],
        compiler_params=_cparams(("arbitrary",)),
        name="ffn",
    )(h2d, h2d, h2d, nw_pre, wg, wu, conv_w, conv_b, wd, nw_post)


def _token_tile(seq):
    return 512 if seq % 512 == 0 else seq


def _encode(x, p):
    batch, seq, _ = x.shape
    tm = _token_tile(seq)
    x2d = x.reshape(batch * seq, D_MODEL)
    qkv, hproj = _inproj(x2d, p["norm_mix_pre"], p["w_in"], _rotary_tables(seq), seq, tm)
    attn = _attention(qkv, batch, seq)
    hg = _hgrn(hproj, p["lb_fwd"], p["lb_bwd"], p["hgrn_out_norm"], batch, seq)
    h = _outproj(attn, hg, x2d, p["w_out"], p["norm_mix_post"], tm)
    y = _ffn(h, p["norm_ffn_pre"], p["w_gate"], p["w_up"], p["conv_w"], p["conv_b"], p["w_down"],
             p["norm_ffn_post"], seq, tm)
    return y.reshape(batch, seq, D_MODEL)


def kernel(x_prompt, x_sample, norm_mix_pre, w_in, hgrn_lb_fwd, hgrn_lb_bwd, hgrn_out_norm, w_out,
           norm_mix_post, norm_ffn_pre, w_gate, w_up, conv_w, conv_b, w_down, norm_ffn_post):
    assert w_in.shape[0] == 1, "one layer"
    p = {
        "norm_mix_pre": norm_mix_pre[0][None, :],
        "w_in": w_in[0].astype(BF16),
        "lb_fwd": hgrn_lb_fwd.astype(F32),
        "lb_bwd": hgrn_lb_bwd.astype(F32),
        "hgrn_out_norm": jnp.tile(hgrn_out_norm[0], LANES // HEAD_DIM)[None, :],
        "w_out": w_out[0].astype(BF16),
        "norm_mix_post": norm_mix_post[0][None, :],
        "norm_ffn_pre": norm_ffn_pre[0][None, :],
        "w_gate": w_gate[0].astype(BF16),
        "w_up": w_up[0].astype(BF16),
        "conv_w": conv_w[0],
        "conv_b": conv_b[0][None, :],
        "w_down": w_down[0].astype(BF16),
        "norm_ffn_post": norm_ffn_post[0][None, :],
    }
    return (_encode(x_prompt, p), _encode(x_sample, p))
```

```python
import functools
import math

import jax
import jax.numpy as jnp
from jax import lax
from jax.experimental import pallas as pl
from jax.experimental.pallas import tpu as pltpu

F32 = jnp.float32
BF16 = jnp.bfloat16

D_MODEL = 1024
HEAD_DIM = 64
ATTN_HEADS = 8
HGRN_HEADS = 8
ATTN_WIDTH = ATTN_HEADS * HEAD_DIM
HGRN_WIDTH = HGRN_HEADS * HEAD_DIM
QKV_WIDTH = 3 * ATTN_WIDTH
HPROJ_WIDTH = 5 * HGRN_WIDTH
IN_WIDTH = QKV_WIDTH + HPROJ_WIDTH
DILATED_BRANCHES = ((128, 1), (512, 4), (2048, 16))
ROPE_THETA = 500000.0
ROT_DIM = HEAD_DIM // 4
ROT_HALF = ROT_DIM // 2
HGRN_CHUNK = 64
D_FF = 2816
NORM_EPS = 1e-6
NEG_INF = -1e30

LANES = 128
SUBLANES = 8
HEAD_PAIRS = ATTN_WIDTH // LANES
VMEM_LIMIT_BYTES = 56 * 1024 * 1024
F32_EXP_SAFE = 80.0


def _cparams(sem):
    return pltpu.CompilerParams(dimension_semantics=sem, vmem_limit_bytes=VMEM_LIMIT_BYTES)


def _rms_scale(x):
    return lax.rsqrt(jnp.mean(x * x, axis=-1, keepdims=True) + NORM_EPS)


def _sigmoid(z):
    return 1.0 / (1.0 + jnp.exp(-z))


def _head_block_ones():
    r = lax.broadcasted_iota(jnp.int32, (LANES, LANES), 0) // HEAD_DIM
    c = lax.broadcasted_iota(jnp.int32, (LANES, LANES), 1) // HEAD_DIM
    return r == c


IN_CHUNK = 512
Q_SCALE = math.log2(math.e) / math.sqrt(HEAD_DIM)


def _inproj_body(x_ref, nw_ref, w_ref, rot_ref, qkv_ref, hp_ref):
    x = x_ref[...]
    xn = (x * _rms_scale(x) * nw_ref[...]).astype(BF16)
    cosv, sin_lo, sin_hi = rot_ref[0], rot_ref[1], rot_ref[2]
    for c in range(IN_WIDTH // IN_CHUNK):
        acc = jnp.dot(xn, w_ref[:, c * IN_CHUNK:(c + 1) * IN_CHUNK], preferred_element_type=F32)
        lo = c * IN_CHUNK
        if lo < 2 * ATTN_WIDTH:
            for g in range(IN_CHUNK // LANES):
                a = acc[:, g * LANES:(g + 1) * LANES]
                r = (a * cosv + pltpu.roll(a, LANES - ROT_HALF, 1) * sin_lo
                     + pltpu.roll(a, ROT_HALF, 1) * sin_hi)
                if lo < ATTN_WIDTH:
                    r = r * Q_SCALE
                qkv_ref[:, lo + g * LANES:lo + (g + 1) * LANES] = r.astype(BF16)
        elif lo < QKV_WIDTH:
            qkv_ref[:, lo:lo + IN_CHUNK] = acc.astype(BF16)
        else:
            hp_ref[:, lo - QKV_WIDTH:lo - QKV_WIDTH + IN_CHUNK] = acc


def _rotary_tables(seq):
    inv_freq = ROPE_THETA ** (-jnp.arange(ROT_HALF, dtype=F32) * 2.0 / ROT_DIM)
    ang = jnp.arange(seq).astype(F32)[:, None] * inv_freq[None, :]
    cos, sin = jnp.cos(ang), jnp.sin(ang)
    zeros = jnp.zeros((seq, HEAD_DIM - ROT_DIM), F32)
    half0 = jnp.zeros((seq, ROT_HALF), F32)
    cos_h = jnp.concatenate([cos, cos, 1.0 + zeros], axis=1)
    lo_h = jnp.concatenate([-sin, half0, zeros], axis=1)
    hi_h = jnp.concatenate([half0, sin, zeros], axis=1)
    per_head = jnp.stack([cos_h, lo_h, hi_h], axis=0)
    return jnp.concatenate([per_head] * (LANES // HEAD_DIM), axis=2)


def _inproj(x2d, norm_w, w_in_bf16, rot, seq, tm):
    tokens = x2d.shape[0]
    tiles_per_seq = seq // tm
    return pl.pallas_call(
        _inproj_body,
        grid=(tokens // tm,),
        in_specs=[
            pl.BlockSpec((tm, D_MODEL), lambda i: (i, 0)),
            pl.BlockSpec((1, D_MODEL), lambda i: (0, 0)),
            pl.BlockSpec((D_MODEL, IN_WIDTH), lambda i: (0, 0), pipeline_mode=pl.Buffered(1)),
            pl.BlockSpec((3, tm, LANES), lambda i: (0, i % tiles_per_seq, 0)),
        ],
        out_specs=[
            pl.BlockSpec((tm, QKV_WIDTH), lambda i: (i, 0)),
            pl.BlockSpec((tm, HPROJ_WIDTH), lambda i: (i, 0)),
        ],
        out_shape=[
            jax.ShapeDtypeStruct((tokens, QKV_WIDTH), BF16),
            jax.ShapeDtypeStruct((tokens, HPROJ_WIDTH), F32),
        ],
        compiler_params=_cparams(("arbitrary",)),
        name="inproj",
    )(x2d, norm_w, w_in_bf16, rot)


ATTN_TQ = 128
ATTN_GROUP = 8
MERGE_ROWS = 256


def _attn_body(q_ref, k_ref, v_ref, out_ref, stage_a, stage_b, perm_ref, bias_ref, o_ref, lse_ref,
               *, seq):
    lane = lax.broadcasted_iota(jnp.int32, (1, LANES), 1)
    head0 = lane < HEAD_DIM

    dils = [d for _, d in DILATED_BRANCHES]
    for ai, src in enumerate((q_ref, k_ref, v_ref)):
        stage_a[...] = src[...].astype(F32)
        stages = (stage_a, stage_b)
        for li in range(1, len(dils)):
            d_prev, dil = dils[li - 1], dils[li]
            ratio = dil // d_prev
            len_prev, sub_len = seq // d_prev, seq // dil
            s_in, s_out = stages[(li - 1) % 2], stages[li % 2]
            dst = perm_ref.at[li - 1, ai]
            keep_f32 = li + 1 < len(dils)

            def gather(t, carry, s_in=s_in, s_out=s_out, dst=dst, d_prev=d_prev, ratio=ratio,
                       len_prev=len_prev, sub_len=sub_len, keep_f32=keep_f32):
                seg, sub = t // ratio, t % ratio
                rows = s_in[pl.ds(seg * len_prev + sub, sub_len, stride=ratio), :]
                out_rows = pl.ds(pl.multiple_of((seg + d_prev * sub) * sub_len, sub_len), sub_len)
                dst[out_rows, :] = rows.astype(BF16)
                if keep_f32:
                    s_out[out_rows, :] = rows
                return carry

            lax.fori_loop(0, dil, gather, 0)

    for bi, (window, dil) in enumerate(DILATED_BRANCHES):
        sub_len = seq // dil
        n_side = window // (2 * dil)
        tq = min(ATTN_TQ, sub_len)
        ks_len = min(sub_len, tq + 2 * n_side)
        nqb = sub_len // tq
        if dil == 1:
            srcs = (q_ref, k_ref, v_ref)
        else:
            srcs = tuple(perm_ref.at[bi - 1, ai] for ai in range(3))

        assert tq == 2 * n_side or nqb == 1
        heads = LANES // HEAD_DIM
        row = lax.broadcasted_iota(jnp.int32, (heads * tq, ks_len), 0) % tq
        col = lax.broadcasted_iota(jnp.int32, (heads * tq, ks_len), 1)
        for di in range(3):
            inside = jnp.abs(row - col + di * n_side) <= n_side
            bias_ref[bi, di, 0:heads * tq, 0:ks_len] = jnp.where(inside, 0.0, NEG_INF)

        def q_group(it, carry, bi=bi, dil=dil, sub_len=sub_len, n_side=n_side, tq=tq,
                    ks_len=ks_len, nqb=nqb, srcs=srcs, heads=heads):
            work = []
            for n in range(ATTN_GROUP):
                f = it * ATTN_GROUP + n
                r = f // nqb
                q0 = (f % nqb) * tq
                base = r * sub_len
                ks = jnp.clip(q0 - n_side, 0, sub_len - ks_len)
                q = srcs[0][pl.ds(pl.multiple_of(base + q0, tq), tq), :]
                k = srcs[1][pl.ds(pl.multiple_of(base + ks, n_side), ks_len), :]
                v = srcs[2][pl.ds(pl.multiple_of(base + ks, n_side), ks_len), :]
                zero = jnp.zeros_like(q)
                qq = jnp.concatenate([jnp.where(head0, q, zero), jnp.where(head0, zero, q)], axis=0)
                s = lax.dot_general(qq, k, (((1,), (1,)), ((), ())), preferred_element_type=F32)
                s = s + bias_ref[bi, (q0 - ks) // n_side, 0:heads * tq, 0:ks_len]
                if dil == 1:
                    rows = pl.ds(pl.multiple_of(q0, tq), tq)
                else:
                    rows = pl.ds(r + q0 * dil, tq, stride=dil)
                work.append((s, v, rows))
            for s, v, rows in work:
                m = jnp.max(s, axis=-1, keepdims=True)
                p = jnp.exp2(s - m)
                l = jnp.sum(p, axis=-1, keepdims=True)
                pv = jnp.dot(p.astype(BF16), v, preferred_element_type=F32)
                l = jnp.where(head0, l[0:tq], l[tq:])
                o_ref.at[bi][rows, :] = jnp.where(head0, pv[0:tq], pv[tq:]) * (1.0 / l)
                lse_ref.at[bi][rows, :] = jnp.where(head0, m[0:tq], m[tq:]) + jnp.log2(l)
            return carry

        lax.fori_loop(0, dil * nqb // ATTN_GROUP, q_group, 0)

    rows_m = min(MERGE_ROWS, seq)

    def merge(i, carry):
        rows = pl.ds(pl.multiple_of(i * rows_m, rows_m), rows_m)
        l1, l2, l3 = lse_ref.at[0][rows, :], lse_ref.at[1][rows, :], lse_ref.at[2][rows, :]
        m = jnp.maximum(jnp.maximum(l1, l2), l3)
        e1, e2, e3 = jnp.exp2(l1 - m), jnp.exp2(l2 - m), jnp.exp2(l3 - m)
        num = e1 * o_ref.at[0][rows, :] + e2 * o_ref.at[1][rows, :] + e3 * o_ref.at[2][rows, :]
        out_ref[rows, :] = (num * (1.0 / (e1 + e2 + e3))).astype(BF16)
        return carry

    lax.fori_loop(0, seq // rows_m, merge, 0)


def _attention(qkv, batch, seq):
    qkv3 = qkv.reshape(batch, seq, QKV_WIDTH)
    nb = len(DILATED_BRANCHES)
    dils = [d for _, d in DILATED_BRANCHES]
    assert dils[0] == 1 and all(b % a == 0 for a, b in zip(dils, dils[1:]))
    n_side = max(w // (2 * d) for w, d in DILATED_BRANCHES)

    def slab(k):
        return pl.BlockSpec((None, seq, LANES), lambda b, p, k=k: (b, 0, k * HEAD_PAIRS + p))

    out = pl.pallas_call(
        functools.partial(_attn_body, seq=seq),
        grid=(batch, HEAD_PAIRS),
        in_specs=[slab(0), slab(1), slab(2)],
        out_specs=pl.BlockSpec((None, seq, LANES), lambda b, p: (b, 0, p)),
        out_shape=jax.ShapeDtypeStruct((batch, seq, ATTN_WIDTH), BF16),
        scratch_shapes=[pltpu.VMEM((seq, LANES), F32),
                        pltpu.VMEM((seq, LANES), F32),
                        pltpu.VMEM((nb - 1, 3, seq, LANES), BF16),
                        pltpu.VMEM((nb, 3, LANES // HEAD_DIM * ATTN_TQ, ATTN_TQ + 2 * n_side), F32),
                        pltpu.VMEM((nb, seq, LANES), F32),
                        pltpu.VMEM((nb, seq, LANES), F32)],
        compiler_params=_cparams(("arbitrary", "arbitrary")),
        name="attention",
    )(qkv3, qkv3, qkv3)
    return out.reshape(batch * seq, ATTN_WIDTH)


HGRN_UNROLL = 16
HGRN_PREP_ROWS = 512


def _chunk_cumsum(g, reverse):
    rows = g.shape[0]
    pos = lax.broadcasted_iota(jnp.int32, (rows, 1), 0) % HGRN_CHUNK
    sh = 1
    while sh < HGRN_CHUNK:
        if reverse:
            shifted = pltpu.roll(g, rows - sh, 0)
            valid = pos < HGRN_CHUNK - sh
        else:
            shifted = pltpu.roll(g, sh, 0)
            valid = pos >= sh
        g = g + jnp.where(valid, shifted, 0.0)
        sh *= 2
    return g


def _gates(zf, lb):
    one_m_lb = 1.0 - lb
    sig = _sigmoid(zf)
    g = jnp.log(lb + one_m_lb * sig)
    kk = one_m_lb * (1.0 - sig)
    return g, kk


def _split_heads(x, head0):
    zero = jnp.zeros_like(x)
    return jnp.concatenate([jnp.where(head0, x, zero), jnp.where(head0, zero, x)], axis=0)


def _causal_keep(reverse):
    t_l = lax.broadcasted_iota(jnp.int32, (HGRN_CHUNK, LANES), 0)
    s_l = lax.broadcasted_iota(jnp.int32, (HGRN_CHUNK, LANES), 1) % HEAD_DIM
    return (s_l >= t_l) if reverse else (s_l <= t_l)


def _hgrn_fast(q_ref, zf_ref, vb_ref, vt_ref, lb, oacc_ref, qt_ref, kt_ref, ke_ref, dec_ref,
               *, seq, reverse):
    cs = HGRN_CHUNK
    n_chunks = seq // cs
    assert HGRN_UNROLL % 2 == 0 and 2 * cs == LANES
    prep = HGRN_PREP_ROWS if seq % HGRN_PREP_ROWS == 0 else cs * SUBLANES
    cpb = prep // cs
    head0 = lax.broadcasted_iota(jnp.int32, (1, LANES), 1) < HEAD_DIM
    same_head = _head_block_ones()
    keep = _causal_keep(reverse)
    pair_low = lax.broadcasted_iota(jnp.int32, (2 * cs, 1), 0) < cs

    def prepare(i, carry):
        rows = pl.ds(pl.multiple_of(i * prep, prep), prep)
        g, kk = _gates(zf_ref[rows, :], lb)
        a = _chunk_cumsum(g, reverse)
        a_end = jnp.sum(g.reshape(cpb, cs, LANES), axis=1)
        ea = jnp.exp(a)
        qt_ref[rows, :] = (q_ref[rows, :] * ea).astype(BF16)
        kt_ref[rows, :] = (kk * (1.0 / ea)).astype(BF16)
        to_end = jnp.exp(a_end[:, None, :] - a.reshape(cpb, cs, LANES)).reshape(prep, LANES)
        ke_ref[rows, :] = (kk * to_end).astype(BF16)
        dec_ref[pl.ds(pl.multiple_of(i * cpb, cpb), cpb), :] = jnp.exp(a_end)
        return carry

    lax.fori_loop(0, seq // prep, prepare, 0)

    def step(i, st):
        chunks = [i * HGRN_UNROLL + u for u in range(HGRN_UNROLL)]
        if reverse:
            chunks = [n_chunks - 1 - j for j in chunks]
        rows = [pl.ds(pl.multiple_of(c * cs, cs), cs) for c in chunks]
        upds = []
        for u, c in enumerate(chunks):
            high = (u % 2 == 0) if reverse else (u % 2 == 1)
            pair = pl.ds(pl.multiple_of((c - int(high)) * cs, 2 * cs), 2 * cs)
            ke_pair = ke_ref[pair, :]
            ke_half = jnp.where(pair_low != high, ke_pair, jnp.zeros_like(ke_pair))
            upds.append(jnp.dot(vt_ref[pair, :], ke_half, preferred_element_type=F32))
        atts = [lax.dot_general(qt_ref[r, :], _split_heads(kt_ref[r, :], head0),
                                (((1,), (1,)), ((), ())), preferred_element_type=F32)
                for r in rows]
        states = []
        for c, upd in zip(chunks, upds):
            states.append(st.astype(BF16))
            st = jnp.where(same_head, st * dec_ref[pl.ds(c, 1), :] + upd, 0.0)
        for r, att, st_in in zip(rows, atts, states):
            att = jnp.where(keep, att, 0.0).astype(BF16)
            o = (jnp.dot(att, _split_heads(vb_ref[r, :], head0), preferred_element_type=F32)
                 + lax.dot_general(qt_ref[r, :], st_in, (((1,), (1,)), ((), ())),
                                   preferred_element_type=F32))
            if reverse:
                oacc_ref[r, :] += o
            else:
                oacc_ref[r, :] = o
        return st

    lax.fori_loop(0, n_chunks // HGRN_UNROLL, step, jnp.zeros((LANES, LANES), F32))


def _hgrn_safe(q_ref, zf_ref, vb_ref, lb, oacc_ref, *, seq, reverse):
    cs = HGRN_CHUNK
    n_chunks = seq // cs
    head0 = lax.broadcasted_iota(jnp.int32, (1, LANES), 1) < HEAD_DIM
    same_head = _head_block_ones()
    keep = _causal_keep(reverse)
    col = lax.broadcasted_iota(jnp.int32, (LANES, LANES), 1)
    last = 0 if reverse else cs - 1

    def chunk(j, st):
        c = n_chunks - 1 - j if reverse else j
        rows = pl.ds(pl.multiple_of(c * cs, cs), cs)
        g, kk = _gates(zf_ref[rows, :], lb)
        q = q_ref[rows, :]
        vb = vb_ref[rows, :]
        a = _chunk_cumsum(g, reverse)
        a_end = a[last:last + 1, :]
        qt = (q * jnp.exp(a)).astype(BF16)

        def column(s, att):
            sel = lax.broadcasted_iota(jnp.int32, (cs, 1), 0) == s
            a_s = jnp.sum(jnp.where(sel, a, 0.0), axis=0, keepdims=True)
            kk_s = jnp.sum(jnp.where(sel, kk, 0.0), axis=0, keepdims=True)
            e = jnp.exp(jnp.minimum(a - a_s, 0.0)) * (q * kk_s)
            place = jnp.logical_and(same_head, col % HEAD_DIM == s).astype(F32)
            return att + jnp.dot(e, place, preferred_element_type=F32,
                                 precision=lax.Precision.HIGHEST)

        att = lax.fori_loop(0, cs, column, jnp.zeros((cs, LANES), F32))
        att = jnp.where(keep, att, 0.0).astype(BF16)
        o = (jnp.dot(att, _split_heads(vb, head0), preferred_element_type=F32)
             + lax.dot_general(qt, st.astype(BF16), (((1,), (1,)), ((), ())),
                               preferred_element_type=F32))
        k_end = (kk * jnp.exp(a_end - a)).astype(BF16)
        upd = lax.dot_general(vb, k_end, (((0,), (0,)), ((), ())), preferred_element_type=F32)
        st = jnp.where(same_head, st * jnp.exp(a_end) + upd, 0.0)
        if reverse:
            oacc_ref[rows, :] += o
        else:
            oacc_ref[rows, :] = o
        return st

    lax.fori_loop(0, n_chunks, chunk, jnp.zeros((LANES, LANES), F32))


def _lower_bound(lbp_ref):
    p = lbp_ref[...]
    e = jnp.exp(p - jnp.max(p, axis=0, keepdims=True))
    return e[0:1, :] / jnp.sum(e, axis=0, keepdims=True)


def _hgrn_body(zq_ref, zff_ref, zfb_ref, v_ref, zg_ref, lbf_ref, lbb_ref, nw_ref, out_ref,
               oacc_ref, q_ref, vb_ref, vt_ref, qt_ref, kt_ref, ke_ref, dec_ref, *, seq):
    rows_n = 512 if seq % 512 == 0 else LANES

    def shared(i, carry):
        rows = pl.ds(pl.multiple_of(i * rows_n, rows_n), rows_n)
        zq = zq_ref[rows, :]
        q_ref[rows, :] = zq * _sigmoid(zq)
        vb_ref[rows, :] = v_ref[rows, :].astype(BF16)
        for t in range(rows_n // LANES):
            tile = pl.ds(pl.multiple_of(i * rows_n + t * LANES, LANES), LANES)
            vt_ref[tile, :] = v_ref[tile, :].T.astype(BF16)
        return carry

    lax.fori_loop(0, seq // rows_n, shared, 0)
    for zf_ref, lbp_ref, reverse in ((zff_ref, lbf_ref, False), (zfb_ref, lbb_ref, True)):
        lb = _lower_bound(lbp_ref)
        worst = jnp.max(-jnp.log(lb)) * HGRN_CHUNK

        @pl.when(worst <= F32_EXP_SAFE)
        def _():
            _hgrn_fast(q_ref, zf_ref, vb_ref, vt_ref, lb, oacc_ref, qt_ref, kt_ref, ke_ref, dec_ref,
                       seq=seq, reverse=reverse)

        @pl.when(jnp.logical_not(worst <= F32_EXP_SAFE))
        def _():
            _hgrn_safe(q_ref, zf_ref, vb_ref, lb, oacc_ref, seq=seq, reverse=reverse)

    ones_blk = _head_block_ones().astype(BF16)

    def finish(i, carry):
        rows = pl.ds(pl.multiple_of(i * rows_n, rows_n), rows_n)
        o = oacc_ref[rows, :]
        sq = o * o
        hi = sq.astype(BF16)
        lo = (sq - hi.astype(F32)).astype(BF16)
        ms = (jnp.dot(hi, ones_blk, preferred_element_type=F32)
              + jnp.dot(lo, ones_blk, preferred_element_type=F32)) * (1.0 / HEAD_DIM)
        zg = zg_ref[rows, :]
        y = o * lax.rsqrt(ms + NORM_EPS) * nw_ref[...] * (zg * _sigmoid(zg))
        out_ref[rows, :] = y.astype(BF16)
        return carry

    lax.fori_loop(0, seq // rows_n, finish, 0)


def _hgrn(hproj, lb_fwd, lb_bwd, out_norm_w, batch, seq):
    hp3 = hproj.reshape(batch, seq, HPROJ_WIDTH)
    nslots = lb_fwd.shape[0]
    cols = HGRN_WIDTH // LANES

    def slab(k):
        return pl.BlockSpec((None, seq, LANES), lambda b, p, k=k: (b, 0, k * cols + p))

    out = pl.pallas_call(
        functools.partial(_hgrn_body, seq=seq),
        grid=(batch, cols),
        in_specs=[slab(0), slab(1), slab(2), slab(3), slab(4),
                  pl.BlockSpec((nslots, LANES), lambda b, p: (0, p)),
                  pl.BlockSpec((nslots, LANES), lambda b, p: (0, p)),
                  pl.BlockSpec((1, LANES), lambda b, p: (0, 0))],
        out_specs=pl.BlockSpec((None, seq, LANES), lambda b, p: (b, 0, p)),
        out_shape=jax.ShapeDtypeStruct((batch, seq, HGRN_WIDTH), BF16),
        scratch_shapes=[pltpu.VMEM((seq, LANES), F32),
                        pltpu.VMEM((seq, LANES), F32),
                        pltpu.VMEM((seq, LANES), BF16),
                        pltpu.VMEM((seq, LANES), BF16),
                        pltpu.VMEM((seq, LANES), BF16),
                        pltpu.VMEM((seq, LANES), BF16),
                        pltpu.VMEM((seq, LANES), BF16),
                        pltpu.VMEM((seq // HGRN_CHUNK, LANES), F32)],
        compiler_params=_cparams(("arbitrary", "arbitrary")),
        name="hgrn",
    )(hp3, hp3, hp3, hp3, hp3, lb_fwd, lb_bwd, out_norm_w)
    return out.reshape(batch * seq, HGRN_WIDTH)


def _outproj_body(attn_ref, hg_ref, x_ref, w_ref, nw_ref, h_ref):
    mix = (jnp.dot(attn_ref[...], w_ref[0:ATTN_WIDTH, :], preferred_element_type=F32)
           + jnp.dot(hg_ref[...], w_ref[ATTN_WIDTH:, :], preferred_element_type=F32))
    h_ref[...] = x_ref[...] + mix * _rms_scale(mix) * nw_ref[...]


def _outproj(attn, hg, x2d, w_out_bf16, norm_w, tm):
    tokens = x2d.shape[0]
    half = pl.BlockSpec((tm, ATTN_WIDTH), lambda i: (i, 0))
    full = pl.BlockSpec((tm, D_MODEL), lambda i: (i, 0))
    return pl.pallas_call(
        _outproj_body,
        grid=(tokens // tm,),
        in_specs=[half, half, full,
                  pl.BlockSpec((D_MODEL, D_MODEL), lambda i: (0, 0), pipeline_mode=pl.Buffered(1)),
                  pl.BlockSpec((1, D_MODEL), lambda i: (0, 0))],
        out_specs=full,
        out_shape=jax.ShapeDtypeStruct((tokens, D_MODEL), F32),
        compiler_params=_cparams(("arbitrary",)),
        name="outproj",
    )(attn, hg, x2d, w_out_bf16, norm_w)


FF_CHUNK = 256
HALO = SUBLANES


def _ffn_body(prev_ref, h_ref, next_ref, nw_pre_ref, wg_ref, wu_ref, cw_ref, cb_ref, wd_ref,
              nw_post_ref, y_ref, g_ref, *, tm, tiles_per_seq):
    i = pl.program_id(0)
    has_prev = (i % tiles_per_seq != 0).astype(F32)
    has_next = (i % tiles_per_seq != tiles_per_seq - 1).astype(F32)
    h = h_ref[...]
    nw = nw_pre_ref[...]

    def normed(t):
        return t * _rms_scale(t) * nw

    xn_main = normed(h).astype(BF16)
    xn_ext = jnp.concatenate([normed(prev_ref[...]) * has_prev, normed(h),
                              normed(next_ref[...]) * has_next], axis=0).astype(BF16)
    ext = tm + 2 * HALO
    for c in range(D_FF // FF_CHUNK):
        cols = slice(c * FF_CHUNK, (c + 1) * FF_CHUNK)
        a = jnp.dot(xn_ext, wg_ref[:, cols], preferred_element_type=F32)
        a_prev = pltpu.roll(a, 1, 0)[HALO:HALO + tm]
        a_next = pltpu.roll(a, ext - 1, 0)[HALO:HALO + tm]
        a_mid = a[HALO:HALO + tm]
        conv = (a_prev * cw_ref[0:1, cols] + a_mid * cw_ref[1:2, cols]
                + a_next * cw_ref[2:3, cols] + cb_ref[:, cols])
        b = jnp.dot(xn_main, wu_ref[:, cols], preferred_element_type=F32)
        gelu = 0.5 * conv * (1.0 + jnp.tanh(math.sqrt(2.0 / math.pi)
                                            * (conv + 0.044715 * (conv * conv * conv))))
        g_ref[:, cols] = (gelu * b).astype(BF16)
    ffn = jnp.dot(g_ref[...], wd_ref[...], preferred_element_type=F32)
    y_ref[...] = h + ffn * _rms_scale(ffn) * nw_post_ref[...]


def _ffn(h2d, nw_pre, wg, wu, conv_w, conv_b, wd, nw_post, seq, tm):
    tokens = h2d.shape[0]
    tiles_per_seq = seq // tm
    hb = tm // HALO
    n_halo = tokens // HALO
    resident = dict(pipeline_mode=pl.Buffered(1))
    return pl.pallas_call(
        functools.partial(_ffn_body, tm=tm, tiles_per_seq=tiles_per_seq),
        grid=(tokens // tm,),
        in_specs=[
            pl.BlockSpec((HALO, D_MODEL), lambda i: (jnp.maximum(i * hb - 1, 0), 0)),
            pl.BlockSpec((tm, D_MODEL), lambda i: (i, 0)),
            pl.BlockSpec((HALO, D_MODEL), lambda i: (jnp.minimum((i + 1) * hb, n_halo - 1), 0)),
            pl.BlockSpec((1, D_MODEL), lambda i: (0, 0)),
            pl.BlockSpec((D_MODEL, D_FF), lambda i: (0, 0), **resident),
            pl.BlockSpec((D_MODEL, D_FF), lambda i: (0, 0), **resident),
            pl.BlockSpec((3, D_FF), lambda i: (0, 0)),
            pl.BlockSpec((1, D_FF), lambda i: (0, 0)),
            pl.BlockSpec((D_FF, D_MODEL), lambda i: (0, 0), **resident),
            pl.BlockSpec((1, D_MODEL), lambda i: (0, 0)),
        ],
        out_specs=pl.BlockSpec((tm, D_MODEL), lambda i: (i, 0)),
        out_shape=jax.ShapeDtypeStruct((tokens, D_MODEL), F32),
        scratch_shapes=[pltpu.VMEM((tm, D_FF), BF16)],
        compiler_params=_cparams(("arbitrary",)),
        name="ffn",
    )(h2d, h2d, h2d, nw_pre, wg, wu, conv_w, conv_b, wd, nw_post)


def _token_tile(seq):
    return 512 if seq % 512 == 0 else seq


def _encode(x, p):
    batch, seq, _ = x.shape
    tm = _token_tile(seq)
    x2d = x.reshape(batch * seq, D_MODEL)
    qkv, hproj = _inproj(x2d, p["norm_mix_pre"], p["w_in"], _rotary_tables(seq), seq, tm)
    attn = _attention(qkv, batch, seq)
    hg = _hgrn(hproj, p["lb_fwd"], p["lb_bwd"], p["hgrn_out_norm"], batch, seq)
    h = _outproj(attn, hg, x2d, p["w_out"], p["norm_mix_post"], tm)
    y = _ffn(h, p["norm_ffn_pre"], p["w_gate"], p["w_up"], p["conv_w"], p["conv_b"], p["w_down"],
             p["norm_ffn_post"], seq, tm)
    return y.reshape(batch, seq, D_MODEL)


def kernel(x_prompt, x_sample, norm_mix_pre, w_in, hgrn_lb_fwd, hgrn_lb_bwd, hgrn_out_norm, w_out,
           norm_mix_post, norm_ffn_pre, w_gate, w_up, conv_w, conv_b, w_down, norm_ffn_post):
    assert w_in.shape[0] == 1, "one layer"
    p = {
        "norm_mix_pre": norm_mix_pre[0][None, :],
        "w_in": w_in[0].astype(BF16),
        "lb_fwd": hgrn_lb_fwd.astype(F32),
        "lb_bwd": hgrn_lb_bwd.astype(F32),
        "hgrn_out_norm": jnp.tile(hgrn_out_norm[0], LANES // HEAD_DIM)[None, :],
        "w_out": w_out[0].astype(BF16),
        "norm_mix_post": norm_mix_post[0][None, :],
        "norm_ffn_pre": norm_ffn_pre[0][None, :],
        "w_gate": w_gate[0].astype(BF16),
        "w_up": w_up[0].astype(BF16),
        "conv_w": conv_w[0],
        "conv_b": conv_b[0][None, :],
        "w_down": w_down[0].astype(BF16),
        "norm_ffn_post": norm_ffn_post[0][None, :],
    }
    return (_encode(x_prompt, p), _encode(x_sample, p))
```

```python
import functools
import math

import jax
import jax.numpy as jnp
from jax import lax
from jax.experimental import pallas as pl
from jax.experimental.pallas import tpu as pltpu

F32 = jnp.float32
BF16 = jnp.bfloat16

D_MODEL = 1024
HEAD_DIM = 64
ATTN_HEADS = 8
HGRN_HEADS = 8
ATTN_WIDTH = ATTN_HEADS * HEAD_DIM
HGRN_WIDTH = HGRN_HEADS * HEAD_DIM
QKV_WIDTH = 3 * ATTN_WIDTH
HPROJ_WIDTH = 5 * HGRN_WIDTH
IN_WIDTH = QKV_WIDTH + HPROJ_WIDTH
DILATED_BRANCHES = ((128, 1), (512, 4), (2048, 16))
ROPE_THETA = 500000.0
ROT_DIM = HEAD_DIM // 4
ROT_HALF = ROT_DIM // 2
HGRN_CHUNK = 64
D_FF = 2816
NORM_EPS = 1e-6
NEG_INF = -1e30

LANES = 128
SUBLANES = 8
HEAD_PAIRS = ATTN_WIDTH // LANES
VMEM_LIMIT_BYTES = 56 * 1024 * 1024
F32_EXP_SAFE = 80.0


def _cparams(sem):
    return pltpu.CompilerParams(dimension_semantics=sem, vmem_limit_bytes=VMEM_LIMIT_BYTES)


def _rms_scale(x):
    return lax.rsqrt(jnp.mean(x * x, axis=-1, keepdims=True) + NORM_EPS)


def _sigmoid(z):
    return 1.0 / (1.0 + jnp.exp(-z))


def _head_block_ones():
    r = lax.broadcasted_iota(jnp.int32, (LANES, LANES), 0) // HEAD_DIM
    c = lax.broadcasted_iota(jnp.int32, (LANES, LANES), 1) // HEAD_DIM
    return r == c


IN_CHUNK = 512
Q_SCALE = math.log2(math.e) / math.sqrt(HEAD_DIM)


def _inproj_body(x_ref, nw_ref, w_ref, rot_ref, qkv_ref, hp_ref):
    x = x_ref[...]
    xn = (x * _rms_scale(x) * nw_ref[...]).astype(BF16)
    cosv, sin_lo, sin_hi = rot_ref[0], rot_ref[1], rot_ref[2]
    for c in range(IN_WIDTH // IN_CHUNK):
        acc = jnp.dot(xn, w_ref[:, c * IN_CHUNK:(c + 1) * IN_CHUNK], preferred_element_type=F32)
        for g in range(IN_CHUNK // LANES):
            lo = c * IN_CHUNK + g * LANES
            a = acc[:, g * LANES:(g + 1) * LANES]
            if lo < 2 * ATTN_WIDTH:
                a = (a * cosv + pltpu.roll(a, LANES - ROT_HALF, 1) * sin_lo
                     + pltpu.roll(a, ROT_HALF, 1) * sin_hi)
                if lo < ATTN_WIDTH:
                    a = a * Q_SCALE
            if lo < QKV_WIDTH:
                qkv_ref[lo // LANES] = a.astype(BF16)
            else:
                hp_ref[(lo - QKV_WIDTH) // LANES] = a


def _rotary_tables(seq):
    inv_freq = ROPE_THETA ** (-jnp.arange(ROT_HALF, dtype=F32) * 2.0 / ROT_DIM)
    ang = jnp.arange(seq).astype(F32)[:, None] * inv_freq[None, :]
    cos, sin = jnp.cos(ang), jnp.sin(ang)
    zeros = jnp.zeros((seq, HEAD_DIM - ROT_DIM), F32)
    half0 = jnp.zeros((seq, ROT_HALF), F32)
    cos_h = jnp.concatenate([cos, cos, 1.0 + zeros], axis=1)
    lo_h = jnp.concatenate([-sin, half0, zeros], axis=1)
    hi_h = jnp.concatenate([half0, sin, zeros], axis=1)
    per_head = jnp.stack([cos_h, lo_h, hi_h], axis=0)
    return jnp.concatenate([per_head] * (LANES // HEAD_DIM), axis=2)


def _inproj(x2d, norm_w, w_in_bf16, rot, batch, seq, tm):
    tiles_per_seq = seq // tm
    n_qkv, n_hp = QKV_WIDTH // LANES, HPROJ_WIDTH // LANES

    def slabs(n):
        return pl.BlockSpec((None, n, tm, LANES),
                            lambda i: (i // tiles_per_seq, 0, i % tiles_per_seq, 0))

    return pl.pallas_call(
        _inproj_body,
        grid=(batch * tiles_per_seq,),
        in_specs=[
            pl.BlockSpec((tm, D_MODEL), lambda i: (i, 0)),
            pl.BlockSpec((1, D_MODEL), lambda i: (0, 0)),
            pl.BlockSpec((D_MODEL, IN_WIDTH), lambda i: (0, 0), pipeline_mode=pl.Buffered(1)),
            pl.BlockSpec((3, tm, LANES), lambda i: (0, i % tiles_per_seq, 0)),
        ],
        out_specs=[slabs(n_qkv), slabs(n_hp)],
        out_shape=[
            jax.ShapeDtypeStruct((batch, n_qkv, seq, LANES), BF16),
            jax.ShapeDtypeStruct((batch, n_hp, seq, LANES), F32),
        ],
        compiler_params=_cparams(("arbitrary",)),
        name="inproj",
    )(x2d, norm_w, w_in_bf16, rot)


ATTN_TQ = 128
ATTN_GROUP = 8
MERGE_ROWS = 256


def _attn_body(q_ref, k_ref, v_ref, out_ref, stage_a, stage_b, perm_ref, bias_ref, o_ref, lse_ref,
               *, seq):
    lane = lax.broadcasted_iota(jnp.int32, (1, LANES), 1)
    head0 = lane < HEAD_DIM

    dils = [d for _, d in DILATED_BRANCHES]
    for ai, src in enumerate((q_ref, k_ref, v_ref)):
        stage_a[...] = src[...].astype(F32)
        stages = (stage_a, stage_b)
        for li in range(1, len(dils)):
            d_prev, dil = dils[li - 1], dils[li]
            ratio = dil // d_prev
            len_prev, sub_len = seq // d_prev, seq // dil
            s_in, s_out = stages[(li - 1) % 2], stages[li % 2]
            dst = perm_ref.at[li - 1, ai]
            keep_f32 = li + 1 < len(dils)

            def gather(t, carry, s_in=s_in, s_out=s_out, dst=dst, d_prev=d_prev, ratio=ratio,
                       len_prev=len_prev, sub_len=sub_len, keep_f32=keep_f32):
                seg, sub = t // ratio, t % ratio
                rows = s_in[pl.ds(seg * len_prev + sub, sub_len, stride=ratio), :]
                out_rows = pl.ds(pl.multiple_of((seg + d_prev * sub) * sub_len, sub_len), sub_len)
                dst[out_rows, :] = rows.astype(BF16)
                if keep_f32:
                    s_out[out_rows, :] = rows
                return carry

            lax.fori_loop(0, dil, gather, 0)

    for bi, (window, dil) in enumerate(DILATED_BRANCHES):
        sub_len = seq // dil
        n_side = window // (2 * dil)
        tq = min(ATTN_TQ, sub_len)
        ks_len = min(sub_len, tq + 2 * n_side)
        nqb = sub_len // tq
        if dil == 1:
            srcs = (q_ref, k_ref, v_ref)
        else:
            srcs = tuple(perm_ref.at[bi - 1, ai] for ai in range(3))

        assert tq == 2 * n_side or nqb == 1
        heads = LANES // HEAD_DIM
        row = lax.broadcasted_iota(jnp.int32, (heads * tq, ks_len), 0) % tq
        col = lax.broadcasted_iota(jnp.int32, (heads * tq, ks_len), 1)
        for di in range(3):
            inside = jnp.abs(row - col + di * n_side) <= n_side
            bias_ref[bi, di, 0:heads * tq, 0:ks_len] = jnp.where(inside, 0.0, NEG_INF)

        def q_group(it, carry, bi=bi, dil=dil, sub_len=sub_len, n_side=n_side, tq=tq,
                    ks_len=ks_len, nqb=nqb, srcs=srcs, heads=heads):
            work = []
            for n in range(ATTN_GROUP):
                f = it * ATTN_GROUP + n
                r = f // nqb
                q0 = (f % nqb) * tq
                base = r * sub_len
                ks = jnp.clip(q0 - n_side, 0, sub_len - ks_len)
                q = srcs[0][pl.ds(pl.multiple_of(base + q0, tq), tq), :]
                k = srcs[1][pl.ds(pl.multiple_of(base + ks, n_side), ks_len), :]
                v = srcs[2][pl.ds(pl.multiple_of(base + ks, n_side), ks_len), :]
                zero = jnp.zeros_like(q)
                qq = jnp.concatenate([jnp.where(head0, q, zero), jnp.where(head0, zero, q)], axis=0)
                s = lax.dot_general(qq, k, (((1,), (1,)), ((), ())), preferred_element_type=F32)
                s = s + bias_ref[bi, (q0 - ks) // n_side, 0:heads * tq, 0:ks_len]
                if dil == 1:
                    rows = pl.ds(pl.multiple_of(q0, tq), tq)
                else:
                    rows = pl.ds(r + q0 * dil, tq, stride=dil)
                work.append((s, v, rows))
            for s, v, rows in work:
                m = jnp.max(s, axis=-1, keepdims=True)
                p = jnp.exp2(s - m)
                l = jnp.sum(p, axis=-1, keepdims=True)
                pv = jnp.dot(p.astype(BF16), v, preferred_element_type=F32)
                l = jnp.where(head0, l[0:tq], l[tq:])
                o_ref.at[bi][rows, :] = jnp.where(head0, pv[0:tq], pv[tq:]) * (1.0 / l)
                lse_ref.at[bi][rows, :] = jnp.where(head0, m[0:tq], m[tq:]) + jnp.log2(l)
            return carry

        lax.fori_loop(0, dil * nqb // ATTN_GROUP, q_group, 0)

    rows_m = min(MERGE_ROWS, seq)

    def merge(i, carry):
        rows = pl.ds(pl.multiple_of(i * rows_m, rows_m), rows_m)
        l1, l2, l3 = lse_ref.at[0][rows, :], lse_ref.at[1][rows, :], lse_ref.at[2][rows, :]
        m = jnp.maximum(jnp.maximum(l1, l2), l3)
        e1, e2, e3 = jnp.exp2(l1 - m), jnp.exp2(l2 - m), jnp.exp2(l3 - m)
        num = e1 * o_ref.at[0][rows, :] + e2 * o_ref.at[1][rows, :] + e3 * o_ref.at[2][rows, :]
        out_ref[rows, :] = (num * (1.0 / (e1 + e2 + e3))).astype(BF16)
        return carry

    lax.fori_loop(0, seq // rows_m, merge, 0)


def _attention(qkv, batch, seq):
    nb = len(DILATED_BRANCHES)
    dils = [d for _, d in DILATED_BRANCHES]
    assert dils[0] == 1 and all(b % a == 0 for a, b in zip(dils, dils[1:]))
    n_side = max(w // (2 * d) for w, d in DILATED_BRANCHES)

    def slab(k):
        return pl.BlockSpec((None, None, seq, LANES),
                            lambda b, p, k=k: (b, k * HEAD_PAIRS + p, 0, 0))

    return pl.pallas_call(
        functools.partial(_attn_body, seq=seq),
        grid=(batch, HEAD_PAIRS),
        in_specs=[slab(0), slab(1), slab(2)],
        out_specs=slab(0),
        out_shape=jax.ShapeDtypeStruct((batch, HEAD_PAIRS, seq, LANES), BF16),
        scratch_shapes=[pltpu.VMEM((seq, LANES), F32),
                        pltpu.VMEM((seq, LANES), F32),
                        pltpu.VMEM((nb - 1, 3, seq, LANES), BF16),
                        pltpu.VMEM((nb, 3, LANES // HEAD_DIM * ATTN_TQ, ATTN_TQ + 2 * n_side), F32),
                        pltpu.VMEM((nb, seq, LANES), F32),
                        pltpu.VMEM((nb, seq, LANES), F32)],
        compiler_params=_cparams(("arbitrary", "arbitrary")),
        name="attention",
    )(qkv, qkv, qkv)


HGRN_UNROLL = 16
HGRN_PREP_ROWS = 512


def _chunk_cumsum(g, reverse):
    rows = g.shape[0]
    pos = lax.broadcasted_iota(jnp.int32, (rows, 1), 0) % HGRN_CHUNK
    sh = 1
    while sh < HGRN_CHUNK:
        if reverse:
            shifted = pltpu.roll(g, rows - sh, 0)
            valid = pos < HGRN_CHUNK - sh
        else:
            shifted = pltpu.roll(g, sh, 0)
            valid = pos >= sh
        g = g + jnp.where(valid, shifted, 0.0)
        sh *= 2
    return g


def _gates(zf, lb):
    one_m_lb = 1.0 - lb
    sig = _sigmoid(zf)
    g = jnp.log(lb + one_m_lb * sig)
    kk = one_m_lb * (1.0 - sig)
    return g, kk


def _split_heads(x, head0):
    zero = jnp.zeros_like(x)
    return jnp.concatenate([jnp.where(head0, x, zero), jnp.where(head0, zero, x)], axis=0)


def _causal_keep(reverse):
    t_l = lax.broadcasted_iota(jnp.int32, (HGRN_CHUNK, LANES), 0)
    s_l = lax.broadcasted_iota(jnp.int32, (HGRN_CHUNK, LANES), 1) % HEAD_DIM
    return (s_l >= t_l) if reverse else (s_l <= t_l)


def _hgrn_fast(q_ref, zf_ref, vb_ref, vt_ref, lb, oacc_ref, qt_ref, kt_ref, ke_ref, dec_ref,
               *, seq, reverse):
    cs = HGRN_CHUNK
    n_chunks = seq // cs
    assert HGRN_UNROLL % 2 == 0 and 2 * cs == LANES
    prep = HGRN_PREP_ROWS if seq % HGRN_PREP_ROWS == 0 else cs * SUBLANES
    cpb = prep // cs
    head0 = lax.broadcasted_iota(jnp.int32, (1, LANES), 1) < HEAD_DIM
    same_head = _head_block_ones()
    keep = _causal_keep(reverse)
    pair_low = lax.broadcasted_iota(jnp.int32, (2 * cs, 1), 0) < cs

    t_i = lax.broadcasted_iota(jnp.int32, (cs, cs), 0)
    s_i = lax.broadcasted_iota(jnp.int32, (cs, cs), 1)
    tri = ((s_i >= t_i) if reverse else (s_i <= t_i)).astype(BF16)

    def cumsum(g):
        hi = g.astype(BF16)
        lo = (g - hi.astype(F32)).astype(BF16)
        split = jnp.concatenate([hi, lo], axis=1)
        parts = []
        for c in range(cpb):
            both = jnp.dot(tri, split[c * cs:(c + 1) * cs], preferred_element_type=F32)
            parts.append(both[:, :LANES] + both[:, LANES:])
        return jnp.concatenate(parts, axis=0)

    def prepare(i, carry):
        rows = pl.ds(pl.multiple_of(i * prep, prep), prep)
        g, kk = _gates(zf_ref[rows, :], lb)
        a = cumsum(g)
        a_end = jnp.sum(g.reshape(cpb, cs, LANES), axis=1)
        ea = jnp.exp(a)
        qt_ref[rows, :] = (q_ref[rows, :] * ea).astype(BF16)
        kt_ref[rows, :] = (kk * (1.0 / ea)).astype(BF16)
        to_end = jnp.exp(a_end[:, None, :] - a.reshape(cpb, cs, LANES)).reshape(prep, LANES)
        ke_ref[rows, :] = (kk * to_end).astype(BF16)
        dec_ref[pl.ds(pl.multiple_of(i * cpb, cpb), cpb), :] = jnp.exp(a_end)
        return carry

    lax.fori_loop(0, seq // prep, prepare, 0)

    def step(i, st):
        chunks = [i * HGRN_UNROLL + u for u in range(HGRN_UNROLL)]
        if reverse:
            chunks = [n_chunks - 1 - j for j in chunks]
        rows = [pl.ds(pl.multiple_of(c * cs, cs), cs) for c in chunks]
        upds = []
        for u, c in enumerate(chunks):
            high = (u % 2 == 0) if reverse else (u % 2 == 1)
            pair = pl.ds(pl.multiple_of((c - int(high)) * cs, 2 * cs), 2 * cs)
            ke_pair = ke_ref[pair, :]
            ke_half = jnp.where(pair_low != high, ke_pair, jnp.zeros_like(ke_pair))
            upds.append(jnp.dot(vt_ref[pair, :], ke_half, preferred_element_type=F32))
        atts = [lax.dot_general(qt_ref[r, :], _split_heads(kt_ref[r, :], head0),
                                (((1,), (1,)), ((), ())), preferred_element_type=F32)
                for r in rows]
        states = []
        for c, upd in zip(chunks, upds):
            states.append(st.astype(BF16))
            st = jnp.where(same_head, st * dec_ref[pl.ds(c, 1), :] + upd, 0.0)
        for r, att, st_in in zip(rows, atts, states):
            att = jnp.where(keep, att, 0.0).astype(BF16)
            o = (jnp.dot(att, _split_heads(vb_ref[r, :], head0), preferred_element_type=F32)
                 + lax.dot_general(qt_ref[r, :], st_in, (((1,), (1,)), ((), ())),
                                   preferred_element_type=F32))
            if reverse:
                oacc_ref[r, :] += o
            else:
                oacc_ref[r, :] = o
        return st

    lax.fori_loop(0, n_chunks // HGRN_UNROLL, step, jnp.zeros((LANES, LANES), F32))


def _hgrn_safe(q_ref, zf_ref, vb_ref, lb, oacc_ref, *, seq, reverse):
    cs = HGRN_CHUNK
    n_chunks = seq // cs
    head0 = lax.broadcasted_iota(jnp.int32, (1, LANES), 1) < HEAD_DIM
    same_head = _head_block_ones()
    keep = _causal_keep(reverse)
    col = lax.broadcasted_iota(jnp.int32, (LANES, LANES), 1)
    last = 0 if reverse else cs - 1

    def chunk(j, st):
        c = n_chunks - 1 - j if reverse else j
        rows = pl.ds(pl.multiple_of(c * cs, cs), cs)
        g, kk = _gates(zf_ref[rows, :], lb)
        q = q_ref[rows, :]
        vb = vb_ref[rows, :]
        a = _chunk_cumsum(g, reverse)
        a_end = a[last:last + 1, :]
        qt = (q * jnp.exp(a)).astype(BF16)

        def column(s, att):
            sel = lax.broadcasted_iota(jnp.int32, (cs, 1), 0) == s
            a_s = jnp.sum(jnp.where(sel, a, 0.0), axis=0, keepdims=True)
            kk_s = jnp.sum(jnp.where(sel, kk, 0.0), axis=0, keepdims=True)
            e = jnp.exp(jnp.minimum(a - a_s, 0.0)) * (q * kk_s)
            place = jnp.logical_and(same_head, col % HEAD_DIM == s).astype(F32)
            return att + jnp.dot(e, place, preferred_element_type=F32,
                                 precision=lax.Precision.HIGHEST)

        att = lax.fori_loop(0, cs, column, jnp.zeros((cs, LANES), F32))
        att = jnp.where(keep, att, 0.0).astype(BF16)
        o = (jnp.dot(att, _split_heads(vb, head0), preferred_element_type=F32)
             + lax.dot_general(qt, st.astype(BF16), (((1,), (1,)), ((), ())),
                               preferred_element_type=F32))
        k_end = (kk * jnp.exp(a_end - a)).astype(BF16)
        upd = lax.dot_general(vb, k_end, (((0,), (0,)), ((), ())), preferred_element_type=F32)
        st = jnp.where(same_head, st * jnp.exp(a_end) + upd, 0.0)
        if reverse:
            oacc_ref[rows, :] += o
        else:
            oacc_ref[rows, :] = o
        return st

    lax.fori_loop(0, n_chunks, chunk, jnp.zeros((LANES, LANES), F32))


def _lower_bound(lbp_ref):
    p = lbp_ref[...]
    e = jnp.exp(p - jnp.max(p, axis=0, keepdims=True))
    return e[0:1, :] / jnp.sum(e, axis=0, keepdims=True)


def _hgrn_body(zq_ref, zff_ref, zfb_ref, v_ref, zg_ref, lbf_ref, lbb_ref, nw_ref, out_ref,
               oacc_ref, q_ref, vb_ref, vt_ref, qt_ref, kt_ref, ke_ref, dec_ref, *, seq):
    rows_n = 512 if seq % 512 == 0 else LANES

    def shared(i, carry):
        rows = pl.ds(pl.multiple_of(i * rows_n, rows_n), rows_n)
        zq = zq_ref[rows, :]
        q_ref[rows, :] = zq * _sigmoid(zq)
        vb_ref[rows, :] = v_ref[rows, :].astype(BF16)
        for t in range(rows_n // LANES):
            tile = pl.ds(pl.multiple_of(i * rows_n + t * LANES, LANES), LANES)
            vt_ref[tile, :] = v_ref[tile, :].T.astype(BF16)
        return carry

    lax.fori_loop(0, seq // rows_n, shared, 0)
    for zf_ref, lbp_ref, reverse in ((zff_ref, lbf_ref, False), (zfb_ref, lbb_ref, True)):
        lb = _lower_bound(lbp_ref)
        worst = jnp.max(-jnp.log(lb)) * HGRN_CHUNK

        @pl.when(worst <= F32_EXP_SAFE)
        def _():
            _hgrn_fast(q_ref, zf_ref, vb_ref, vt_ref, lb, oacc_ref, qt_ref, kt_ref, ke_ref, dec_ref,
                       seq=seq, reverse=reverse)

        @pl.when(jnp.logical_not(worst <= F32_EXP_SAFE))
        def _():
            _hgrn_safe(q_ref, zf_ref, vb_ref, lb, oacc_ref, seq=seq, reverse=reverse)

    ones_blk = _head_block_ones().astype(BF16)

    def finish(i, carry):
        rows = pl.ds(pl.multiple_of(i * rows_n, rows_n), rows_n)
        o = oacc_ref[rows, :]
        sq = o * o
        hi = sq.astype(BF16)
        lo = (sq - hi.astype(F32)).astype(BF16)
        ms = (jnp.dot(hi, ones_blk, preferred_element_type=F32)
              + jnp.dot(lo, ones_blk, preferred_element_type=F32)) * (1.0 / HEAD_DIM)
        zg = zg_ref[rows, :]
        y = o * lax.rsqrt(ms + NORM_EPS) * nw_ref[...] * (zg * _sigmoid(zg))
        out_ref[rows, :] = y.astype(BF16)
        return carry

    lax.fori_loop(0, seq // rows_n, finish, 0)


def _hgrn(hproj, lb_fwd, lb_bwd, out_norm_w, batch, seq):
    nslots = lb_fwd.shape[0]
    cols = HGRN_WIDTH // LANES

    def slab(k):
        return pl.BlockSpec((None, None, seq, LANES), lambda b, p, k=k: (b, k * cols + p, 0, 0))

    return pl.pallas_call(
        functools.partial(_hgrn_body, seq=seq),
        grid=(batch, cols),
        in_specs=[slab(0), slab(1), slab(2), slab(3), slab(4),
                  pl.BlockSpec((nslots, LANES), lambda b, p: (0, p)),
                  pl.BlockSpec((nslots, LANES), lambda b, p: (0, p)),
                  pl.BlockSpec((1, LANES), lambda b, p: (0, 0))],
        out_specs=slab(0),
        out_shape=jax.ShapeDtypeStruct((batch, cols, seq, LANES), BF16),
        scratch_shapes=[pltpu.VMEM((seq, LANES), F32),
                        pltpu.VMEM((seq, LANES), F32),
                        pltpu.VMEM((seq, LANES), BF16),
                        pltpu.VMEM((seq, LANES), BF16),
                        pltpu.VMEM((seq, LANES), BF16),
                        pltpu.VMEM((seq, LANES), BF16),
                        pltpu.VMEM((seq, LANES), BF16),
                        pltpu.VMEM((seq // HGRN_CHUNK, LANES), F32)],
        compiler_params=_cparams(("arbitrary", "arbitrary")),
        name="hgrn",
    )(hproj, hproj, hproj, hproj, hproj, lb_fwd, lb_bwd, out_norm_w)


FF_CHUNK = 256
HALO = 16


def _mix_ffn_body(ap_ref, a_ref, an_ref, gp_ref, g_ref, gn_ref, xp_ref, x_ref, xn_ref,
                  wo_ref, nw_mix_ref, nw_pre_ref, wg_ref, wu_ref, cw_ref, cb_ref, wd_ref,
                  nw_post_ref, y_ref, act_ref, *, tm, tiles_per_seq):
    i = pl.program_id(0)
    ext = tm + 2 * HALO

    def mixer_rows(attn_ref, hgrn_ref):
        return jnp.concatenate([attn_ref[j] for j in range(attn_ref.shape[0])]
                               + [hgrn_ref[j] for j in range(hgrn_ref.shape[0])], axis=1)

    mixed = jnp.concatenate([mixer_rows(ap_ref, gp_ref), mixer_rows(a_ref, g_ref),
                             mixer_rows(an_ref, gn_ref)], axis=0)
    mix = jnp.dot(mixed, wo_ref[...], preferred_element_type=F32)
    x_ext = jnp.concatenate([xp_ref[...], x_ref[...], xn_ref[...]], axis=0)
    h_ext = x_ext + mix * _rms_scale(mix) * nw_mix_ref[...]
    h = h_ext[HALO:HALO + tm]

    row = lax.broadcasted_iota(jnp.int32, (ext, 1), 0)
    has_prev = (i % tiles_per_seq != 0).astype(F32)
    has_next = (i % tiles_per_seq != tiles_per_seq - 1).astype(F32)
    live = jnp.where(row < HALO, has_prev, jnp.where(row >= HALO + tm, has_next, 1.0))
    xn = h_ext * _rms_scale(h_ext) * nw_pre_ref[...] * live
    xn_ext = xn.astype(BF16)
    xn_main = xn[HALO:HALO + tm].astype(BF16)
    for c in range(D_FF // FF_CHUNK):
        cols = slice(c * FF_CHUNK, (c + 1) * FF_CHUNK)
        a = jnp.dot(xn_ext, wg_ref[:, cols], preferred_element_type=F32)
        a_prev = pltpu.roll(a, 1, 0)[HALO:HALO + tm]
        a_next = pltpu.roll(a, ext - 1, 0)[HALO:HALO + tm]
        a_mid = a[HALO:HALO + tm]
        conv = (a_prev * cw_ref[0:1, cols] + a_mid * cw_ref[1:2, cols]
                + a_next * cw_ref[2:3, cols] + cb_ref[:, cols])
        b = jnp.dot(xn_main, wu_ref[:, cols], preferred_element_type=F32)
        gelu = 0.5 * conv * (1.0 + jnp.tanh(math.sqrt(2.0 / math.pi)
                                            * (conv + 0.044715 * (conv * conv * conv))))
        act_ref[:, cols] = (gelu * b).astype(BF16)
    ffn = jnp.dot(act_ref[...], wd_ref[...], preferred_element_type=F32)
    y_ref[...] = h + ffn * _rms_scale(ffn) * nw_post_ref[...]


def _mix_ffn(attn, hg, x2d, p, seq, tm):
    tokens = x2d.shape[0]
    tiles_per_seq = seq // tm
    hb = tm // HALO
    halos_per_seq = seq // HALO
    n_halo = tokens // HALO
    resident = dict(pipeline_mode=pl.Buffered(1))

    def slabs(rows, row_block):
        return pl.BlockSpec((None, HEAD_PAIRS, rows, LANES),
                            lambda i: (i // tiles_per_seq, 0, row_block(i % tiles_per_seq), 0))

    def prev_block(t):
        return jnp.maximum(t * hb - 1, 0)

    def next_block(t):
        return jnp.minimum((t + 1) * hb, halos_per_seq - 1)

    head_group = [slabs(HALO, prev_block), slabs(tm, lambda t: t), slabs(HALO, next_block)]
    row_vec = lambda n: pl.BlockSpec((1, n), lambda i: (0, 0))
    return pl.pallas_call(
        functools.partial(_mix_ffn_body, tm=tm, tiles_per_seq=tiles_per_seq),
        grid=(tokens // tm,),
        in_specs=head_group + head_group + [
            pl.BlockSpec((HALO, D_MODEL), lambda i: (jnp.maximum(i * hb - 1, 0), 0)),
            pl.BlockSpec((tm, D_MODEL), lambda i: (i, 0)),
            pl.BlockSpec((HALO, D_MODEL), lambda i: (jnp.minimum((i + 1) * hb, n_halo - 1), 0)),
            pl.BlockSpec((D_MODEL, D_MODEL), lambda i: (0, 0), **resident),
            row_vec(D_MODEL),
            row_vec(D_MODEL),
            pl.BlockSpec((D_MODEL, D_FF), lambda i: (0, 0), **resident),
            pl.BlockSpec((D_MODEL, D_FF), lambda i: (0, 0), **resident),
            pl.BlockSpec((3, D_FF), lambda i: (0, 0)),
            row_vec(D_FF),
            pl.BlockSpec((D_FF, D_MODEL), lambda i: (0, 0), **resident),
            row_vec(D_MODEL),
        ],
        out_specs=pl.BlockSpec((tm, D_MODEL), lambda i: (i, 0)),
        out_shape=jax.ShapeDtypeStruct((tokens, D_MODEL), F32),
        scratch_shapes=[pltpu.VMEM((tm, D_FF), BF16)],
        compiler_params=_cparams(("arbitrary",)),
        name="mix_ffn",
    )(attn, attn, attn, hg, hg, hg, x2d, x2d, x2d, p["w_out"], p["norm_mix_post"],
      p["norm_ffn_pre"], p["w_gate"], p["w_up"], p["conv_w"], p["conv_b"], p["w_down"],
      p["norm_ffn_post"])


def _token_tile(seq):
    return 512 if seq % 512 == 0 else seq


def _encode(x, p):
    batch, seq, _ = x.shape
    tm = _token_tile(seq)
    x2d = x.reshape(batch * seq, D_MODEL)
    qkv, hproj = _inproj(x2d, p["norm_mix_pre"], p["w_in"], _rotary_tables(seq), batch, seq, tm)
    attn = _attention(qkv, batch, seq)
    hg = _hgrn(hproj, p["lb_fwd"], p["lb_bwd"], p["hgrn_out_norm"], batch, seq)
    y = _mix_ffn(attn, hg, x2d, p, seq, tm)
    return y.reshape(batch, seq, D_MODEL)


def kernel(x_prompt, x_sample, norm_mix_pre, w_in, hgrn_lb_fwd, hgrn_lb_bwd, hgrn_out_norm, w_out,
           norm_mix_post, norm_ffn_pre, w_gate, w_up, conv_w, conv_b, w_down, norm_ffn_post):
    assert w_in.shape[0] == 1, "one layer"
    p = {
        "norm_mix_pre": norm_mix_pre[0][None, :],
        "w_in": w_in[0].astype(BF16),
        "lb_fwd": hgrn_lb_fwd.astype(F32),
        "lb_bwd": hgrn_lb_bwd.astype(F32),
        "hgrn_out_norm": jnp.tile(hgrn_out_norm[0], LANES // HEAD_DIM)[None, :],
        "w_out": w_out[0].astype(BF16),
        "norm_mix_post": norm_mix_post[0][None, :],
        "norm_ffn_pre": norm_ffn_pre[0][None, :],
        "w_gate": w_gate[0].astype(BF16),
        "w_up": w_up[0].astype(BF16),
        "conv_w": conv_w[0],
        "conv_b": conv_b[0][None, :],
        "w_down": w_down[0].astype(BF16),
        "norm_ffn_post": norm_ffn_post[0][None, :],
    }
    return (_encode(x_prompt, p), _encode(x_sample, p))
```

```python
import functools
import math

import jax
import jax.numpy as jnp
from jax import lax
from jax.experimental import pallas as pl
from jax.experimental.pallas import tpu as pltpu

F32 = jnp.float32
BF16 = jnp.bfloat16

D_MODEL = 1024
HEAD_DIM = 64
ATTN_HEADS = 8
HGRN_HEADS = 8
ATTN_WIDTH = ATTN_HEADS * HEAD_DIM
HGRN_WIDTH = HGRN_HEADS * HEAD_DIM
QKV_WIDTH = 3 * ATTN_WIDTH
HPROJ_WIDTH = 5 * HGRN_WIDTH
IN_WIDTH = QKV_WIDTH + HPROJ_WIDTH
DILATED_BRANCHES = ((128, 1), (512, 4), (2048, 16))
ROPE_THETA = 500000.0
ROT_DIM = HEAD_DIM // 4
ROT_HALF = ROT_DIM // 2
HGRN_CHUNK = 64
D_FF = 2816
NORM_EPS = 1e-6
NEG_INF = -1e30

LANES = 128
SUBLANES = 8
HEAD_PAIRS = ATTN_WIDTH // LANES
VMEM_LIMIT_BYTES = 56 * 1024 * 1024
F32_EXP_SAFE = 80.0


def _cparams(sem):
    return pltpu.CompilerParams(dimension_semantics=sem, vmem_limit_bytes=VMEM_LIMIT_BYTES)


def _rms_scale(x):
    return lax.rsqrt(jnp.mean(x * x, axis=-1, keepdims=True) + NORM_EPS)


def _sigmoid(z):
    return 1.0 / (1.0 + jnp.exp(-z))


def _head_block_ones():
    r = lax.broadcasted_iota(jnp.int32, (LANES, LANES), 0) // HEAD_DIM
    c = lax.broadcasted_iota(jnp.int32, (LANES, LANES), 1) // HEAD_DIM
    return r == c


IN_CHUNK = 256
Q_SCALE = math.log2(math.e) / math.sqrt(HEAD_DIM)


def _lower_bound(lbp_ref):
    p = lbp_ref[...]
    e = jnp.exp(p - jnp.max(p, axis=0, keepdims=True))
    return e[0:1, :] / jnp.sum(e, axis=0, keepdims=True)


def _inproj_body(x_ref, nw_ref, w_ref, rot_ref, lbf_ref, lbb_ref, *refs, dils):
    qkv_ref = refs[0]
    perm_refs = refs[1:len(dils)]
    hf_ref, hb_ref, tile_a, tile_b = refs[len(dils):]
    tm = x_ref.shape[0]
    x = x_ref[...]
    xn = (x * _rms_scale(x) * nw_ref[...]).astype(BF16)
    cosv, sin_lo, sin_hi = rot_ref[0], rot_ref[1], rot_ref[2]
    lbs = (_lower_bound(lbf_ref), _lower_bound(lbb_ref))
    n_chunks = IN_WIDTH // IN_CHUNK
    i_chunk = (QKV_WIDTH + 3 * HGRN_WIDTH) // IN_CHUNK
    order = [c for c in range(n_chunks) if c * IN_CHUNK >= QKV_WIDTH and c != i_chunk]
    order += [c for c in range(n_chunks) if c * IN_CHUNK < QKV_WIDTH] + [i_chunk]
    for c in order:
        acc = jnp.dot(xn, w_ref[:, c * IN_CHUNK:(c + 1) * IN_CHUNK], preferred_element_type=F32)
        for g in range(IN_CHUNK // LANES):
            lo = c * IN_CHUNK + g * LANES
            a = acc[:, g * LANES:(g + 1) * LANES]
            if lo < QKV_WIDTH:
                if lo < 2 * ATTN_WIDTH:
                    a = (a * cosv + pltpu.roll(a, LANES - ROT_HALF, 1) * sin_lo
                         + pltpu.roll(a, ROT_HALF, 1) * sin_hi)
                    if lo < ATTN_WIDTH:
                        a = a * Q_SCALE
                slab = lo // LANES
                qkv_ref[slab] = a.astype(BF16)
                tile_a[...] = a
                tiles = (tile_a, tile_b)
                for li in range(1, len(dils)):
                    d_prev, dil = dils[li - 1], dils[li]
                    ratio, len_prev, sub = dil // d_prev, tm // d_prev, tm // dil
                    t_in, t_out = tiles[(li - 1) % 2], tiles[li % 2]
                    for seg in range(d_prev):
                        for k in range(ratio):
                            rows = t_in[pl.ds(seg * len_prev + k, sub, stride=ratio), :]
                            res = seg + d_prev * k
                            perm_refs[li - 1][slab, res] = rows.astype(BF16)
                            if li + 1 < len(dils):
                                t_out[res * sub:(res + 1) * sub, :] = rows
            else:
                group, pair = divmod((lo - QKV_WIDTH) // LANES, HGRN_WIDTH // LANES)
                if group == 0 or group == 4:
                    val = a * _sigmoid(a)
                    if group == 0:
                        hf_ref[pair] = val
                    else:
                        hb_ref[3 * (HGRN_WIDTH // LANES) + pair] = val.astype(BF16)
                elif group == 3:
                    hb_ref[2 * (HGRN_WIDTH // LANES) + pair] = a.astype(BF16)
                else:
                    lb = lbs[group - 1][:, pair * LANES:(pair + 1) * LANES]
                    sig = _sigmoid(a)
                    hf_ref[group * (HGRN_WIDTH // LANES) + pair] = jnp.log(lb + (1.0 - lb) * sig)
                    hb_ref[(group - 1) * (HGRN_WIDTH // LANES) + pair] = (
                        (1.0 - lb) * (1.0 - sig)).astype(BF16)


def _rotary_tables(seq):
    inv_freq = ROPE_THETA ** (-jnp.arange(ROT_HALF, dtype=F32) * 2.0 / ROT_DIM)
    ang = jnp.arange(seq).astype(F32)[:, None] * inv_freq[None, :]
    cos, sin = jnp.cos(ang), jnp.sin(ang)
    zeros = jnp.zeros((seq, HEAD_DIM - ROT_DIM), F32)
    half0 = jnp.zeros((seq, ROT_HALF), F32)
    cos_h = jnp.concatenate([cos, cos, 1.0 + zeros], axis=1)
    lo_h = jnp.concatenate([-sin, half0, zeros], axis=1)
    hi_h = jnp.concatenate([half0, sin, zeros], axis=1)
    per_head = jnp.stack([cos_h, lo_h, hi_h], axis=0)
    return jnp.concatenate([per_head] * (LANES // HEAD_DIM), axis=2)


def _inproj(x2d, norm_w, w_in_bf16, rot, lb_fwd, lb_bwd, batch, seq, tm):
    tiles_per_seq = seq // tm
    n_qkv = QKV_WIDTH // LANES
    n_pairs = HGRN_WIDTH // LANES
    nslots = lb_fwd.shape[0]
    dils = tuple(d for _, d in DILATED_BRANCHES)
    assert dils[0] == 1 and all(b % a == 0 for a, b in zip(dils, dils[1:]))

    def slabs(n):
        return pl.BlockSpec((None, n, tm, LANES),
                            lambda i: (i // tiles_per_seq, 0, i % tiles_per_seq, 0))

    def residues(d):
        return pl.BlockSpec((None, n_qkv, d, tm // d, LANES),
                            lambda i: (i // tiles_per_seq, 0, 0, i % tiles_per_seq, 0))

    outs = pl.pallas_call(
        functools.partial(_inproj_body, dils=dils),
        grid=(batch * tiles_per_seq,),
        in_specs=[
            pl.BlockSpec((tm, D_MODEL), lambda i: (i, 0)),
            pl.BlockSpec((1, D_MODEL), lambda i: (0, 0)),
            pl.BlockSpec((D_MODEL, IN_WIDTH), lambda i: (0, 0), pipeline_mode=pl.Buffered(1)),
            pl.BlockSpec((3, tm, LANES), lambda i: (0, i % tiles_per_seq, 0)),
            pl.BlockSpec((nslots, HGRN_WIDTH), lambda i: (0, 0)),
            pl.BlockSpec((nslots, HGRN_WIDTH), lambda i: (0, 0)),
        ],
        out_specs=([slabs(n_qkv)] + [residues(d) for d in dils[1:]]
                   + [slabs(3 * n_pairs), slabs(4 * n_pairs)]),
        out_shape=([jax.ShapeDtypeStruct((batch, n_qkv, seq, LANES), BF16)]
                   + [jax.ShapeDtypeStruct((batch, n_qkv, d, seq // d, LANES), BF16)
                      for d in dils[1:]]
                   + [jax.ShapeDtypeStruct((batch, 3 * n_pairs, seq, LANES), F32),
                      jax.ShapeDtypeStruct((batch, 4 * n_pairs, seq, LANES), BF16)]),
        scratch_shapes=[pltpu.VMEM((tm, LANES), F32), pltpu.VMEM((tm, LANES), F32)],
        compiler_params=_cparams(("arbitrary",)),
        name="inproj",
    )(x2d, norm_w, w_in_bf16, rot, lb_fwd, lb_bwd)
    qkv_orders = [outs[0]] + [o.reshape(batch, n_qkv, seq, LANES) for o in outs[1:len(dils)]]
    return qkv_orders, outs[len(dils)], outs[len(dils) + 1]


ATTN_TQ = 128
ATTN_GROUP = 8
MERGE_ROWS = 256


def _attn_body(*refs, seq):
    nb = len(DILATED_BRANCHES)
    qkv_refs = [refs[3 * bi:3 * bi + 3] for bi in range(nb)]
    out_ref, bias_ref, o_ref, lse_ref = refs[3 * nb:]
    lane = lax.broadcasted_iota(jnp.int32, (1, LANES), 1)
    head0 = lane < HEAD_DIM

    for bi, (window, dil) in enumerate(DILATED_BRANCHES):
        sub_len = seq // dil
        n_side = window // (2 * dil)
        tq = min(ATTN_TQ, sub_len)
        ks_len = min(sub_len, tq + 2 * n_side)
        nqb = sub_len // tq
        srcs = qkv_refs[bi]

        assert tq == 2 * n_side or nqb == 1
        heads = LANES // HEAD_DIM
        row = lax.broadcasted_iota(jnp.int32, (heads * tq, ks_len), 0) % tq
        col = lax.broadcasted_iota(jnp.int32, (heads * tq, ks_len), 1)
        for di in range(3):
            inside = jnp.abs(row - col + di * n_side) <= n_side
            bias_ref[bi, di, 0:heads * tq, 0:ks_len] = jnp.where(inside, 0.0, NEG_INF)

        def q_group(it, carry, bi=bi, dil=dil, sub_len=sub_len, n_side=n_side, tq=tq,
                    ks_len=ks_len, nqb=nqb, srcs=srcs, heads=heads):
            work = []
            for n in range(ATTN_GROUP):
                f = it * ATTN_GROUP + n
                r = f // nqb
                q0 = (f % nqb) * tq
                base = r * sub_len
                ks = jnp.clip(q0 - n_side, 0, sub_len - ks_len)
                q = srcs[0][pl.ds(pl.multiple_of(base + q0, tq), tq), :]
                k = srcs[1][pl.ds(pl.multiple_of(base + ks, n_side), ks_len), :]
                v = srcs[2][pl.ds(pl.multiple_of(base + ks, n_side), ks_len), :]
                zero = jnp.zeros_like(q)
                qq = jnp.concatenate([jnp.where(head0, q, zero), jnp.where(head0, zero, q)], axis=0)
                s = lax.dot_general(qq, k, (((1,), (1,)), ((), ())), preferred_element_type=F32)
                s = s + bias_ref[bi, (q0 - ks) // n_side, 0:heads * tq, 0:ks_len]
                if dil == 1:
                    rows = pl.ds(pl.multiple_of(q0, tq), tq)
                else:
                    rows = pl.ds(r + q0 * dil, tq, stride=dil)
                work.append((s, v, rows))
            for s, v, rows in work:
                m = jnp.max(s, axis=-1, keepdims=True)
                p = jnp.exp2(s - m)
                l = jnp.sum(p, axis=-1, keepdims=True)
                pv = jnp.dot(p.astype(BF16), v, preferred_element_type=F32)
                l = jnp.where(head0, l[0:tq], l[tq:])
                o_ref.at[bi][rows, :] = jnp.where(head0, pv[0:tq], pv[tq:]) * (1.0 / l)
                lse_ref.at[bi][rows, :] = jnp.where(head0, m[0:tq], m[tq:]) + jnp.log2(l)
            return carry

        lax.fori_loop(0, dil * nqb // ATTN_GROUP, q_group, 0)

    rows_m = min(MERGE_ROWS, seq)

    def merge(i, carry):
        rows = pl.ds(pl.multiple_of(i * rows_m, rows_m), rows_m)
        l1, l2, l3 = lse_ref.at[0][rows, :], lse_ref.at[1][rows, :], lse_ref.at[2][rows, :]
        m = jnp.maximum(jnp.maximum(l1, l2), l3)
        e1, e2, e3 = jnp.exp2(l1 - m), jnp.exp2(l2 - m), jnp.exp2(l3 - m)
        num = e1 * o_ref.at[0][rows, :] + e2 * o_ref.at[1][rows, :] + e3 * o_ref.at[2][rows, :]
        out_ref[rows, :] = (num * (1.0 / (e1 + e2 + e3))).astype(BF16)
        return carry

    lax.fori_loop(0, seq // rows_m, merge, 0)


def _attention(qkv_orders, batch, seq):
    nb = len(DILATED_BRANCHES)
    n_side = max(w // (2 * d) for w, d in DILATED_BRANCHES)

    def slab(k):
        return pl.BlockSpec((None, None, seq, LANES),
                            lambda b, p, k=k: (b, k * HEAD_PAIRS + p, 0, 0))

    operands = [arr for arr in qkv_orders for _ in range(3)]
    return pl.pallas_call(
        functools.partial(_attn_body, seq=seq),
        grid=(batch, HEAD_PAIRS),
        in_specs=[slab(k) for _ in qkv_orders for k in range(3)],
        out_specs=slab(0),
        out_shape=jax.ShapeDtypeStruct((batch, HEAD_PAIRS, seq, LANES), BF16),
        scratch_shapes=[pltpu.VMEM((nb, 3, LANES // HEAD_DIM * ATTN_TQ, ATTN_TQ + 2 * n_side), F32),
                        pltpu.VMEM((nb, seq, LANES), F32),
                        pltpu.VMEM((nb, seq, LANES), F32)],
        compiler_params=_cparams(("arbitrary", "arbitrary")),
        name="attention",
    )(*operands)


HGRN_UNROLL = 16
HGRN_PREP_ROWS = 512


def _chunk_cumsum(g, reverse):
    rows = g.shape[0]
    pos = lax.broadcasted_iota(jnp.int32, (rows, 1), 0) % HGRN_CHUNK
    sh = 1
    while sh < HGRN_CHUNK:
        if reverse:
            shifted = pltpu.roll(g, rows - sh, 0)
            valid = pos < HGRN_CHUNK - sh
        else:
            shifted = pltpu.roll(g, sh, 0)
            valid = pos >= sh
        g = g + jnp.where(valid, shifted, 0.0)
        sh *= 2
    return g


def _split_heads(x, head0):
    zero = jnp.zeros_like(x)
    return jnp.concatenate([jnp.where(head0, x, zero), jnp.where(head0, zero, x)], axis=0)


def _causal_keep(reverse):
    t_l = lax.broadcasted_iota(jnp.int32, (HGRN_CHUNK, LANES), 0)
    s_l = lax.broadcasted_iota(jnp.int32, (HGRN_CHUNK, LANES), 1) % HEAD_DIM
    return (s_l >= t_l) if reverse else (s_l <= t_l)


def _hgrn_fast(q_ref, g_ref, kk_ref, vb_ref, vt_ref, oacc_ref, qt_ref, kt_ref, ke_ref, dec_ref,
               *, seq, reverse):
    cs = HGRN_CHUNK
    n_chunks = seq // cs
    assert HGRN_UNROLL % 2 == 0 and 2 * cs == LANES
    prep = HGRN_PREP_ROWS if seq % HGRN_PREP_ROWS == 0 else cs * SUBLANES
    cpb = prep // cs
    head0 = lax.broadcasted_iota(jnp.int32, (1, LANES), 1) < HEAD_DIM
    same_head = _head_block_ones()
    keep = _causal_keep(reverse)
    pair_low = lax.broadcasted_iota(jnp.int32, (2 * cs, 1), 0) < cs

    t_i = lax.broadcasted_iota(jnp.int32, (cs, cs), 0)
    s_i = lax.broadcasted_iota(jnp.int32, (cs, cs), 1)
    tri = ((s_i >= t_i) if reverse else (s_i <= t_i)).astype(BF16)

    def cumsum(g):
        hi = g.astype(BF16)
        lo = (g - hi.astype(F32)).astype(BF16)
        split = jnp.concatenate([hi, lo], axis=1)
        parts = []
        for c in range(cpb):
            both = jnp.dot(tri, split[c * cs:(c + 1) * cs], preferred_element_type=F32)
            parts.append(both[:, :LANES] + both[:, LANES:])
        return jnp.concatenate(parts, axis=0)

    def prepare(i, carry):
        rows = pl.ds(pl.multiple_of(i * prep, prep), prep)
        g = g_ref[rows, :]
        kk = kk_ref[rows, :].astype(F32)
        a = cumsum(g)
        a_end = jnp.sum(g.reshape(cpb, cs, LANES), axis=1)
        ea = jnp.exp(a)
        qt_ref[rows, :] = (q_ref[rows, :] * ea).astype(BF16)
        kt_ref[rows, :] = (kk * (1.0 / ea)).astype(BF16)
        to_end = jnp.exp(a_end[:, None, :] - a.reshape(cpb, cs, LANES)).reshape(prep, LANES)
        ke_ref[rows, :] = (kk * to_end).astype(BF16)
        dec_ref[pl.ds(pl.multiple_of(i * cpb, cpb), cpb), :] = jnp.exp(a_end)
        return carry

    lax.fori_loop(0, seq // prep, prepare, 0)

    def step(i, st):
        chunks = [i * HGRN_UNROLL + u for u in range(HGRN_UNROLL)]
        if reverse:
            chunks = [n_chunks - 1 - j for j in chunks]
        rows = [pl.ds(pl.multiple_of(c * cs, cs), cs) for c in chunks]
        upds = []
        for u, c in enumerate(chunks):
            high = (u % 2 == 0) if reverse else (u % 2 == 1)
            pair = pl.ds(pl.multiple_of((c - int(high)) * cs, 2 * cs), 2 * cs)
            ke_pair = ke_ref[pair, :]
            ke_half = jnp.where(pair_low != high, ke_pair, jnp.zeros_like(ke_pair))
            upds.append(jnp.dot(vt_ref[pair, :], ke_half, preferred_element_type=F32))
        atts = [lax.dot_general(qt_ref[r, :], _split_heads(kt_ref[r, :], head0),
                                (((1,), (1,)), ((), ())), preferred_element_type=F32)
                for r in rows]
        states = []
        for c, upd in zip(chunks, upds):
            states.append(st.astype(BF16))
            st = jnp.where(same_head, st * dec_ref[pl.ds(c, 1), :] + upd, 0.0)
        for r, att, st_in in zip(rows, atts, states):
            att = jnp.where(keep, att, 0.0).astype(BF16)
            o = (jnp.dot(att, _split_heads(vb_ref[r, :], head0), preferred_element_type=F32)
                 + lax.dot_general(qt_ref[r, :], st_in, (((1,), (1,)), ((), ())),
                                   preferred_element_type=F32))
            if reverse:
                oacc_ref[r, :] += o
            else:
                oacc_ref[r, :] = o
        return st

    lax.fori_loop(0, n_chunks // HGRN_UNROLL, step, jnp.zeros((LANES, LANES), F32))


def _hgrn_safe(q_ref, g_ref, kk_ref, vb_ref, oacc_ref, *, seq, reverse):
    cs = HGRN_CHUNK
    n_chunks = seq // cs
    head0 = lax.broadcasted_iota(jnp.int32, (1, LANES), 1) < HEAD_DIM
    same_head = _head_block_ones()
    keep = _causal_keep(reverse)
    col = lax.broadcasted_iota(jnp.int32, (LANES, LANES), 1)
    last = 0 if reverse else cs - 1

    def chunk(j, st):
        c = n_chunks - 1 - j if reverse else j
        rows = pl.ds(pl.multiple_of(c * cs, cs), cs)
        g = g_ref[rows, :]
        kk = kk_ref[rows, :].astype(F32)
        q = q_ref[rows, :]
        vb = vb_ref[rows, :]
        a = _chunk_cumsum(g, reverse)
        a_end = a[last:last + 1, :]
        qt = (q * jnp.exp(a)).astype(BF16)

        def column(s, att):
            sel = lax.broadcasted_iota(jnp.int32, (cs, 1), 0) == s
            a_s = jnp.sum(jnp.where(sel, a, 0.0), axis=0, keepdims=True)
            kk_s = jnp.sum(jnp.where(sel, kk, 0.0), axis=0, keepdims=True)
            e = jnp.exp(jnp.minimum(a - a_s, 0.0)) * (q * kk_s)
            place = jnp.logical_and(same_head, col % HEAD_DIM == s).astype(F32)
            return att + jnp.dot(e, place, preferred_element_type=F32,
                                 precision=lax.Precision.HIGHEST)

        att = lax.fori_loop(0, cs, column, jnp.zeros((cs, LANES), F32))
        att = jnp.where(keep, att, 0.0).astype(BF16)
        o = (jnp.dot(att, _split_heads(vb, head0), preferred_element_type=F32)
             + lax.dot_general(qt, st.astype(BF16), (((1,), (1,)), ((), ())),
                               preferred_element_type=F32))
        k_end = (kk * jnp.exp(a_end - a)).astype(BF16)
        upd = lax.dot_general(vb, k_end, (((0,), (0,)), ((), ())), preferred_element_type=F32)
        st = jnp.where(same_head, st * jnp.exp(a_end) + upd, 0.0)
        if reverse:
            oacc_ref[rows, :] += o
        else:
            oacc_ref[rows, :] = o
        return st

    lax.fori_loop(0, n_chunks, chunk, jnp.zeros((LANES, LANES), F32))


def _hgrn_body(q_ref, gf_ref, gb_ref, kf_ref, kb_ref, vb_ref, gate_ref, lbf_ref, lbb_ref, nw_ref,
               out_ref, oacc_ref, vt_ref, qt_ref, kt_ref, ke_ref, dec_ref, *, seq):
    rows_n = 512 if seq % 512 == 0 else LANES

    def transpose_v(i, carry):
        for t in range(rows_n // LANES):
            tile = pl.ds(pl.multiple_of(i * rows_n + t * LANES, LANES), LANES)
            vt_ref[tile, :] = vb_ref[tile, :].astype(F32).T.astype(BF16)
        return carry

    lax.fori_loop(0, seq // rows_n, transpose_v, 0)
    for g_ref, kk_ref, lbp_ref, reverse in ((gf_ref, kf_ref, lbf_ref, False),
                                            (gb_ref, kb_ref, lbb_ref, True)):
        worst = jnp.max(-jnp.log(_lower_bound(lbp_ref))) * HGRN_CHUNK

        @pl.when(worst <= F32_EXP_SAFE)
        def _():
            _hgrn_fast(q_ref, g_ref, kk_ref, vb_ref, vt_ref, oacc_ref, qt_ref, kt_ref, ke_ref,
                       dec_ref, seq=seq, reverse=reverse)

        @pl.when(jnp.logical_not(worst <= F32_EXP_SAFE))
        def _():
            _hgrn_safe(q_ref, g_ref, kk_ref, vb_ref, oacc_ref, seq=seq, reverse=reverse)

    ones_blk = _head_block_ones().astype(BF16)

    def finish(i, carry):
        rows = pl.ds(pl.multiple_of(i * rows_n, rows_n), rows_n)
        o = oacc_ref[rows, :]
        sq = o * o
        hi = sq.astype(BF16)
        lo = (sq - hi.astype(F32)).astype(BF16)
        ms = (jnp.dot(hi, ones_blk, preferred_element_type=F32)
              + jnp.dot(lo, ones_blk, preferred_element_type=F32)) * (1.0 / HEAD_DIM)
        y = o * lax.rsqrt(ms + NORM_EPS) * nw_ref[...] * gate_ref[rows, :].astype(F32)
        out_ref[rows, :] = y.astype(BF16)
        return carry

    lax.fori_loop(0, seq // rows_n, finish, 0)


def _hgrn(hf, hb, lb_fwd, lb_bwd, out_norm_w, batch, seq):
    nslots = lb_fwd.shape[0]
    cols = HGRN_WIDTH // LANES

    def slab(k):
        return pl.BlockSpec((None, None, seq, LANES), lambda b, p, k=k: (b, k * cols + p, 0, 0))

    return pl.pallas_call(
        functools.partial(_hgrn_body, seq=seq),
        grid=(batch, cols),
        in_specs=[slab(0), slab(1), slab(2), slab(0), slab(1), slab(2), slab(3),
                  pl.BlockSpec((nslots, LANES), lambda b, p: (0, p)),
                  pl.BlockSpec((nslots, LANES), lambda b, p: (0, p)),
                  pl.BlockSpec((1, LANES), lambda b, p: (0, 0))],
        out_specs=slab(0),
        out_shape=jax.ShapeDtypeStruct((batch, cols, seq, LANES), BF16),
        scratch_shapes=[pltpu.VMEM((seq, LANES), F32),
                        pltpu.VMEM((seq, LANES), BF16),
                        pltpu.VMEM((seq, LANES), BF16),
                        pltpu.VMEM((seq, LANES), BF16),
                        pltpu.VMEM((seq, LANES), BF16),
                        pltpu.VMEM((seq // HGRN_CHUNK, LANES), F32)],
        compiler_params=_cparams(("arbitrary", "arbitrary")),
        name="hgrn",
    )(hf, hf, hf, hb, hb, hb, hb, lb_fwd, lb_bwd, out_norm_w)


FF_CHUNK = 256
HALO = 16


def _mix_ffn_body(ap_ref, a_ref, an_ref, gp_ref, g_ref, gn_ref, xp_ref, x_ref, xn_ref,
                  wo_ref, nw_mix_ref, nw_pre_ref, wg_ref, wu_ref, cw_ref, cb_ref, wd_ref,
                  nw_post_ref, y_ref, act_ref, *, tm, tiles_per_seq):
    i = pl.program_id(0)
    ext = tm + 2 * HALO

    def mixer_rows(attn_ref, hgrn_ref):
        return jnp.concatenate([attn_ref[j] for j in range(attn_ref.shape[0])]
                               + [hgrn_ref[j] for j in range(hgrn_ref.shape[0])], axis=1)

    mixed = jnp.concatenate([mixer_rows(ap_ref, gp_ref), mixer_rows(a_ref, g_ref),
                             mixer_rows(an_ref, gn_ref)], axis=0)
    mix = jnp.dot(mixed, wo_ref[...], preferred_element_type=F32)
    x_ext = jnp.concatenate([xp_ref[...], x_ref[...], xn_ref[...]], axis=0)
    h_ext = x_ext + mix * _rms_scale(mix) * nw_mix_ref[...]
    h = h_ext[HALO:HALO + tm]

    row = lax.broadcasted_iota(jnp.int32, (ext, 1), 0)
    has_prev = (i % tiles_per_seq != 0).astype(F32)
    has_next = (i % tiles_per_seq != tiles_per_seq - 1).astype(F32)
    live = jnp.where(row < HALO, has_prev, jnp.where(row >= HALO + tm, has_next, 1.0))
    xn = h_ext * _rms_scale(h_ext) * nw_pre_ref[...] * live
    xn_ext = xn.astype(BF16)
    xn_main = xn[HALO:HALO + tm].astype(BF16)
    for c in range(D_FF // FF_CHUNK):
        cols = slice(c * FF_CHUNK, (c + 1) * FF_CHUNK)
        a = jnp.dot(xn_ext, wg_ref[:, cols], preferred_element_type=F32)
        a_prev = pltpu.roll(a, 1, 0)[HALO:HALO + tm]
        a_next = pltpu.roll(a, ext - 1, 0)[HALO:HALO + tm]
        a_mid = a[HALO:HALO + tm]
        conv = (a_prev * cw_ref[0:1, cols] + a_mid * cw_ref[1:2, cols]
                + a_next * cw_ref[2:3, cols] + cb_ref[:, cols])
        b = jnp.dot(xn_main, wu_ref[:, cols], preferred_element_type=F32)
        gelu = 0.5 * conv * (1.0 + jnp.tanh(math.sqrt(2.0 / math.pi)
                                            * (conv + 0.044715 * (conv * conv * conv))))
        act_ref[:, cols] = (gelu * b).astype(BF16)
    ffn = jnp.dot(act_ref[...], wd_ref[...], preferred_element_type=F32)
    y_ref[...] = h + ffn * _rms_scale(ffn) * nw_post_ref[...]


def _mix_ffn(attn, hg, x2d, p, seq, tm):
    tokens = x2d.shape[0]
    tiles_per_seq = seq // tm
    hb = tm // HALO
    halos_per_seq = seq // HALO
    n_halo = tokens // HALO
    resident = dict(pipeline_mode=pl.Buffered(1))

    def slabs(rows, row_block):
        return pl.BlockSpec((None, HEAD_PAIRS, rows, LANES),
                            lambda i: (i // tiles_per_seq, 0, row_block(i % tiles_per_seq), 0))

    def prev_block(t):
        return jnp.maximum(t * hb - 1, 0)

    def next_block(t):
        return jnp.minimum((t + 1) * hb, halos_per_seq - 1)

    head_group = [slabs(HALO, prev_block), slabs(tm, lambda t: t), slabs(HALO, next_block)]
    row_vec = lambda n: pl.BlockSpec((1, n), lambda i: (0, 0))
    return pl.pallas_call(
        functools.partial(_mix_ffn_body, tm=tm, tiles_per_seq=tiles_per_seq),
        grid=(tokens // tm,),
        in_specs=head_group + head_group + [
            pl.BlockSpec((HALO, D_MODEL), lambda i: (jnp.maximum(i * hb - 1, 0), 0)),
            pl.BlockSpec((tm, D_MODEL), lambda i: (i, 0)),
            pl.BlockSpec((HALO, D_MODEL), lambda i: (jnp.minimum((i + 1) * hb, n_halo - 1), 0)),
            pl.BlockSpec((D_MODEL, D_MODEL), lambda i: (0, 0), **resident),
            row_vec(D_MODEL),
            row_vec(D_MODEL),
            pl.BlockSpec((D_MODEL, D_FF), lambda i: (0, 0), **resident),
            pl.BlockSpec((D_MODEL, D_FF), lambda i: (0, 0), **resident),
            pl.BlockSpec((3, D_FF), lambda i: (0, 0)),
            row_vec(D_FF),
            pl.BlockSpec((D_FF, D_MODEL), lambda i: (0, 0), **resident),
            row_vec(D_MODEL),
        ],
        out_specs=pl.BlockSpec((tm, D_MODEL), lambda i: (i, 0)),
        out_shape=jax.ShapeDtypeStruct((tokens, D_MODEL), F32),
        scratch_shapes=[pltpu.VMEM((tm, D_FF), BF16)],
        compiler_params=_cparams(("arbitrary",)),
        name="mix_ffn",
    )(attn, attn, attn, hg, hg, hg, x2d, x2d, x2d, p["w_out"], p["norm_mix_post"],
      p["norm_ffn_pre"], p["w_gate"], p["w_up"], p["conv_w"], p["conv_b"], p["w_down"],
      p["norm_ffn_post"])


def _token_tile(seq):
    return 512 if seq % 512 == 0 else seq


def _encode(x, p):
    batch, seq, _ = x.shape
    tm = _token_tile(seq)
    x2d = x.reshape(batch * seq, D_MODEL)
    qkv_orders, hf, hb = _inproj(x2d, p["norm_mix_pre"], p["w_in"], _rotary_tables(seq),
                                 p["lb_fwd"], p["lb_bwd"], batch, seq, tm)
    attn = _attention(qkv_orders, batch, seq)
    hg = _hgrn(hf, hb, p["lb_fwd"], p["lb_bwd"], p["hgrn_out_norm"], batch, seq)
    y = _mix_ffn(attn, hg, x2d, p, seq, tm)
    return y.reshape(batch, seq, D_MODEL)


def kernel(x_prompt, x_sample, norm_mix_pre, w_in, hgrn_lb_fwd, hgrn_lb_bwd, hgrn_out_norm, w_out,
           norm_mix_post, norm_ffn_pre, w_gate, w_up, conv_w, conv_b, w_down, norm_ffn_post):
    assert w_in.shape[0] == 1, "one layer"
    p = {
        "norm_mix_pre": norm_mix_pre[0][None, :],
        "w_in": w_in[0].astype(BF16),
        "lb_fwd": hgrn_lb_fwd.astype(F32),
        "lb_bwd": hgrn_lb_bwd.astype(F32),
        "hgrn_out_norm": jnp.tile(hgrn_out_norm[0], LANES // HEAD_DIM)[None, :],
        "w_out": w_out[0].astype(BF16),
        "norm_mix_post": norm_mix_post[0][None, :],
        "norm_ffn_pre": norm_ffn_pre[0][None, :],
        "w_gate": w_gate[0].astype(BF16),
        "w_up": w_up[0].astype(BF16),
        "conv_w": conv_w[0],
        "conv_b": conv_b[0][None, :],
        "w_down": w_down[0].astype(BF16),
        "norm_ffn_post": norm_ffn_post[0][None, :],
    }
    return (_encode(x_prompt, p), _encode(x_sample, p))
```

```python
import functools
import math

import jax
import jax.numpy as jnp
from jax import lax
from jax.experimental import pallas as pl
from jax.experimental.pallas import tpu as pltpu

F32 = jnp.float32
BF16 = jnp.bfloat16

D_MODEL = 1024
HEAD_DIM = 64
ATTN_HEADS = 8
HGRN_HEADS = 8
ATTN_WIDTH = ATTN_HEADS * HEAD_DIM
HGRN_WIDTH = HGRN_HEADS * HEAD_DIM
QKV_WIDTH = 3 * ATTN_WIDTH
HPROJ_WIDTH = 5 * HGRN_WIDTH
IN_WIDTH = QKV_WIDTH + HPROJ_WIDTH
DILATED_BRANCHES = ((128, 1), (512, 4), (2048, 16))
ROPE_THETA = 500000.0
ROT_DIM = HEAD_DIM // 4
ROT_HALF = ROT_DIM // 2
HGRN_CHUNK = 64
D_FF = 2816
NORM_EPS = 1e-6
NEG_INF = -1e30

LANES = 128
SUBLANES = 8
HEAD_PAIRS = ATTN_WIDTH // LANES
VMEM_LIMIT_BYTES = 56 * 1024 * 1024
F32_EXP_SAFE = 80.0


def _cparams(sem):
    return pltpu.CompilerParams(dimension_semantics=sem, vmem_limit_bytes=VMEM_LIMIT_BYTES)


def _rms_scale(x):
    return lax.rsqrt(jnp.mean(x * x, axis=-1, keepdims=True) + NORM_EPS)


def _sigmoid(z):
    return 1.0 / (1.0 + jnp.exp(-z))


def _head_block_ones():
    r = lax.broadcasted_iota(jnp.int32, (LANES, LANES), 0) // HEAD_DIM
    c = lax.broadcasted_iota(jnp.int32, (LANES, LANES), 1) // HEAD_DIM
    return r == c


IN_CHUNK = 256
IN_ROWS = 128
Q_SCALE = math.log2(math.e) / math.sqrt(HEAD_DIM)


def _lower_bound(lbp_ref):
    p = lbp_ref[...]
    e = jnp.exp(p - jnp.max(p, axis=0, keepdims=True))
    return e[0:1, :] / jnp.sum(e, axis=0, keepdims=True)


def _inproj_body(x_ref, nw_ref, w_ref, rot_ref, lbf_ref, lbb_ref, *refs, dils):
    qkv_ref = refs[0]
    perm_refs = refs[1:len(dils)]
    hf_ref, hb_ref, tile_a, tile_b = refs[len(dils):]
    tm = x_ref.shape[0]
    x = x_ref[...]
    xn = (x * _rms_scale(x) * nw_ref[...]).astype(BF16)
    lbs = (_lower_bound(lbf_ref), _lower_bound(lbb_ref))
    n_pairs = HGRN_WIDTH // LANES
    n_chunks = IN_WIDTH // IN_CHUNK
    i_chunk = (QKV_WIDTH + 3 * HGRN_WIDTH) // IN_CHUNK
    order = [c for c in range(n_chunks) if c * IN_CHUNK >= QKV_WIDTH and c != i_chunk]
    order += [c for c in range(n_chunks) if c * IN_CHUNK < QKV_WIDTH] + [i_chunk]
    for c in order:
        acc = jnp.dot(xn, w_ref[:, c * IN_CHUNK:(c + 1) * IN_CHUNK], preferred_element_type=F32)
        for g in range(IN_CHUNK // LANES):
            lo = c * IN_CHUNK + g * LANES
            slab = lo // LANES
            group, pair = divmod((lo - QKV_WIDTH) // LANES, n_pairs)
            for rb in range(tm // IN_ROWS):
                rs = slice(rb * IN_ROWS, (rb + 1) * IN_ROWS)
                a = acc[rs, g * LANES:(g + 1) * LANES]
                if lo < QKV_WIDTH:
                    if lo < 2 * ATTN_WIDTH:
                        a = (a * rot_ref[0, rs, :] + pltpu.roll(a, LANES - ROT_HALF, 1) * rot_ref[1, rs, :]
                             + pltpu.roll(a, ROT_HALF, 1) * rot_ref[2, rs, :])
                        if lo < ATTN_WIDTH:
                            a = a * Q_SCALE
                    qkv_ref[slab, rs, :] = a.astype(BF16)
                    tile_a[rs, :] = a
                elif group == 0:
                    hf_ref[pair, rs, :] = a * _sigmoid(a)
                elif group == 4:
                    hb_ref[3 * n_pairs + pair, rs, :] = (a * _sigmoid(a)).astype(BF16)
                elif group == 3:
                    hb_ref[2 * n_pairs + pair, rs, :] = a.astype(BF16)
                else:
                    lb = lbs[group - 1][:, pair * LANES:(pair + 1) * LANES]
                    sig = _sigmoid(a)
                    hf_ref[group * n_pairs + pair, rs, :] = jnp.log(lb + (1.0 - lb) * sig)
                    hb_ref[(group - 1) * n_pairs + pair, rs, :] = (
                        (1.0 - lb) * (1.0 - sig)).astype(BF16)
            if lo < QKV_WIDTH:
                tiles = (tile_a, tile_b)
                for li in range(1, len(dils)):
                    d_prev, dil = dils[li - 1], dils[li]
                    ratio, len_prev, sub = dil // d_prev, tm // d_prev, tm // dil
                    t_in, t_out = tiles[(li - 1) % 2], tiles[li % 2]
                    for seg in range(d_prev):
                        for k in range(ratio):
                            rows = t_in[pl.ds(seg * len_prev + k, sub, stride=ratio), :]
                            res = seg + d_prev * k
                            perm_refs[li - 1][slab, res] = rows.astype(BF16)
                            if li + 1 < len(dils):
                                t_out[res * sub:(res + 1) * sub, :] = rows


def _rotary_tables(seq):
    inv_freq = ROPE_THETA ** (-jnp.arange(ROT_HALF, dtype=F32) * 2.0 / ROT_DIM)
    ang = jnp.arange(seq).astype(F32)[:, None] * inv_freq[None, :]
    cos, sin = jnp.cos(ang), jnp.sin(ang)
    zeros = jnp.zeros((seq, HEAD_DIM - ROT_DIM), F32)
    half0 = jnp.zeros((seq, ROT_HALF), F32)
    cos_h = jnp.concatenate([cos, cos, 1.0 + zeros], axis=1)
    lo_h = jnp.concatenate([-sin, half0, zeros], axis=1)
    hi_h = jnp.concatenate([half0, sin, zeros], axis=1)
    per_head = jnp.stack([cos_h, lo_h, hi_h], axis=0)
    return jnp.concatenate([per_head] * (LANES // HEAD_DIM), axis=2)


def _inproj(x2d, norm_w, w_in_bf16, rot, lb_fwd, lb_bwd, batch, seq, tm):
    tiles_per_seq = seq // tm
    n_qkv = QKV_WIDTH // LANES
    n_pairs = HGRN_WIDTH // LANES
    nslots = lb_fwd.shape[0]
    dils = tuple(d for _, d in DILATED_BRANCHES)
    assert dils[0] == 1 and all(b % a == 0 for a, b in zip(dils, dils[1:]))

    def slabs(n):
        return pl.BlockSpec((None, n, tm, LANES),
                            lambda i: (i // tiles_per_seq, 0, i % tiles_per_seq, 0))

    def residues(d):
        return pl.BlockSpec((None, n_qkv, d, tm // d, LANES),
                            lambda i: (i // tiles_per_seq, 0, 0, i % tiles_per_seq, 0))

    outs = pl.pallas_call(
        functools.partial(_inproj_body, dils=dils),
        grid=(batch * tiles_per_seq,),
        in_specs=[
            pl.BlockSpec((tm, D_MODEL), lambda i: (i, 0)),
            pl.BlockSpec((1, D_MODEL), lambda i: (0, 0)),
            pl.BlockSpec((D_MODEL, IN_WIDTH), lambda i: (0, 0), pipeline_mode=pl.Buffered(1)),
            pl.BlockSpec((3, tm, LANES), lambda i: (0, i % tiles_per_seq, 0)),
            pl.BlockSpec((nslots, HGRN_WIDTH), lambda i: (0, 0)),
            pl.BlockSpec((nslots, HGRN_WIDTH), lambda i: (0, 0)),
        ],
        out_specs=([slabs(n_qkv)] + [residues(d) for d in dils[1:]]
                   + [slabs(3 * n_pairs), slabs(4 * n_pairs)]),
        out_shape=([jax.ShapeDtypeStruct((batch, n_qkv, seq, LANES), BF16)]
                   + [jax.ShapeDtypeStruct((batch, n_qkv, d, seq // d, LANES), BF16)
                      for d in dils[1:]]
                   + [jax.ShapeDtypeStruct((batch, 3 * n_pairs, seq, LANES), F32),
                      jax.ShapeDtypeStruct((batch, 4 * n_pairs, seq, LANES), BF16)]),
        scratch_shapes=[pltpu.VMEM((tm, LANES), F32), pltpu.VMEM((tm, LANES), F32)],
        compiler_params=_cparams(("arbitrary",)),
        name="inproj",
    )(x2d, norm_w, w_in_bf16, rot, lb_fwd, lb_bwd)
    qkv_orders = [outs[0]] + [o.reshape(batch, n_qkv, seq, LANES) for o in outs[1:len(dils)]]
    return qkv_orders, outs[len(dils)], outs[len(dils) + 1]


ATTN_TQ = 128
ATTN_GROUP = 8
MERGE_ROWS = 256


def _attn_body(*refs, seq):
    nb = len(DILATED_BRANCHES)
    qkv_refs = [refs[3 * bi:3 * bi + 3] for bi in range(nb)]
    out_ref, bias_ref, o_ref, lse_ref = refs[3 * nb:]
    lane = lax.broadcasted_iota(jnp.int32, (1, LANES), 1)
    head0 = lane < HEAD_DIM

    for bi, (window, dil) in enumerate(DILATED_BRANCHES):
        sub_len = seq // dil
        n_side = window // (2 * dil)
        tq = min(ATTN_TQ, sub_len)
        ks_len = min(sub_len, tq + 2 * n_side)
        nqb = sub_len // tq
        srcs = qkv_refs[bi]

        assert tq == 2 * n_side or nqb == 1
        heads = LANES // HEAD_DIM
        row = lax.broadcasted_iota(jnp.int32, (heads * tq, ks_len), 0) % tq
        col = lax.broadcasted_iota(jnp.int32, (heads * tq, ks_len), 1)
        for di in range(3):
            inside = jnp.abs(row - col + di * n_side) <= n_side
            bias_ref[bi, di, 0:heads * tq, 0:ks_len] = jnp.where(inside, 0.0, NEG_INF)

        def q_group(it, carry, bi=bi, dil=dil, sub_len=sub_len, n_side=n_side, tq=tq,
                    ks_len=ks_len, nqb=nqb, srcs=srcs, heads=heads):
            work = []
            for n in range(ATTN_GROUP):
                f = it * ATTN_GROUP + n
                r = f // nqb
                q0 = (f % nqb) * tq
                base = r * sub_len
                ks = jnp.clip(q0 - n_side, 0, sub_len - ks_len)
                q = srcs[0][pl.ds(pl.multiple_of(base + q0, tq), tq), :]
                k = srcs[1][pl.ds(pl.multiple_of(base + ks, n_side), ks_len), :]
                v = srcs[2][pl.ds(pl.multiple_of(base + ks, n_side), ks_len), :]
                zero = jnp.zeros_like(q)
                qq = jnp.concatenate([jnp.where(head0, q, zero), jnp.where(head0, zero, q)], axis=0)
                s = lax.dot_general(qq, k, (((1,), (1,)), ((), ())), preferred_element_type=F32)
                s = s + bias_ref[bi, (q0 - ks) // n_side, 0:heads * tq, 0:ks_len]
                if dil == 1:
                    rows = pl.ds(pl.multiple_of(q0, tq), tq)
                else:
                    rows = pl.ds(r + q0 * dil, tq, stride=dil)
                work.append((s, v, rows))
            for s, v, rows in work:
                m = jnp.max(s, axis=-1, keepdims=True)
                p = jnp.exp2(s - m)
                l = jnp.sum(p, axis=-1, keepdims=True)
                pv = jnp.dot(p.astype(BF16), v, preferred_element_type=F32)
                l = jnp.where(head0, l[0:tq], l[tq:])
                o_ref.at[bi][rows, :] = jnp.where(head0, pv[0:tq], pv[tq:]) * (1.0 / l)
                lse_ref.at[bi][rows, :] = jnp.where(head0, m[0:tq], m[tq:]) + jnp.log2(l)
            return carry

        lax.fori_loop(0, dil * nqb // ATTN_GROUP, q_group, 0)

    rows_m = min(MERGE_ROWS, seq)

    def merge(i, carry):
        rows = pl.ds(pl.multiple_of(i * rows_m, rows_m), rows_m)
        l1, l2, l3 = lse_ref.at[0][rows, :], lse_ref.at[1][rows, :], lse_ref.at[2][rows, :]
        m = jnp.maximum(jnp.maximum(l1, l2), l3)
        e1, e2, e3 = jnp.exp2(l1 - m), jnp.exp2(l2 - m), jnp.exp2(l3 - m)
        num = e1 * o_ref.at[0][rows, :] + e2 * o_ref.at[1][rows, :] + e3 * o_ref.at[2][rows, :]
        out_ref[rows, :] = (num * (1.0 / (e1 + e2 + e3))).astype(BF16)
        return carry

    lax.fori_loop(0, seq // rows_m, merge, 0)


def _attention(qkv_orders, batch, seq):
    nb = len(DILATED_BRANCHES)
    n_side = max(w // (2 * d) for w, d in DILATED_BRANCHES)

    def slab(k):
        return pl.BlockSpec((None, None, seq, LANES),
                            lambda b, p, k=k: (b, k * HEAD_PAIRS + p, 0, 0))

    operands = [arr for arr in qkv_orders for _ in range(3)]
    return pl.pallas_call(
        functools.partial(_attn_body, seq=seq),
        grid=(batch, HEAD_PAIRS),
        in_specs=[slab(k) for _ in qkv_orders for k in range(3)],
        out_specs=slab(0),
        out_shape=jax.ShapeDtypeStruct((batch, HEAD_PAIRS, seq, LANES), BF16),
        scratch_shapes=[pltpu.VMEM((nb, 3, LANES // HEAD_DIM * ATTN_TQ, ATTN_TQ + 2 * n_side), F32),
                        pltpu.VMEM((nb, seq, LANES), F32),
                        pltpu.VMEM((nb, seq, LANES), F32)],
        compiler_params=_cparams(("arbitrary", "arbitrary")),
        name="attention",
    )(*operands)


HGRN_UNROLL = 16
HGRN_PREP_ROWS = 1024


def _chunk_cumsum(g, reverse):
    rows = g.shape[0]
    pos = lax.broadcasted_iota(jnp.int32, (rows, 1), 0) % HGRN_CHUNK
    sh = 1
    while sh < HGRN_CHUNK:
        if reverse:
            shifted = pltpu.roll(g, rows - sh, 0)
            valid = pos < HGRN_CHUNK - sh
        else:
            shifted = pltpu.roll(g, sh, 0)
            valid = pos >= sh
        g = g + jnp.where(valid, shifted, 0.0)
        sh *= 2
    return g


def _split_heads(x, head0):
    zero = jnp.zeros_like(x)
    return jnp.concatenate([jnp.where(head0, x, zero), jnp.where(head0, zero, x)], axis=0)


def _causal_keep(reverse):
    t_l = lax.broadcasted_iota(jnp.int32, (HGRN_CHUNK, LANES), 0)
    s_l = lax.broadcasted_iota(jnp.int32, (HGRN_CHUNK, LANES), 1) % HEAD_DIM
    return (s_l >= t_l) if reverse else (s_l <= t_l)


def _hgrn_fast(q_ref, g_ref, kk_ref, vb_ref, vt_ref, oacc_ref, qt_ref, kt_ref, ke_ref, dec_ref,
               *, seq, reverse):
    cs = HGRN_CHUNK
    n_chunks = seq // cs
    prep = HGRN_PREP_ROWS if seq % HGRN_PREP_ROWS == 0 else cs * SUBLANES
    cpb = prep // cs
    head0 = lax.broadcasted_iota(jnp.int32, (1, LANES), 1) < HEAD_DIM
    same_head = _head_block_ones()
    t_h = lax.broadcasted_iota(jnp.int32, (2 * cs, cs), 0) % cs
    s_h = lax.broadcasted_iota(jnp.int32, (2 * cs, cs), 1)
    keep = (s_h >= t_h) if reverse else (s_h <= t_h)

    t_i = lax.broadcasted_iota(jnp.int32, (cs, cs), 0)
    s_i = lax.broadcasted_iota(jnp.int32, (cs, cs), 1)
    tri = ((s_i >= t_i) if reverse else (s_i <= t_i)).astype(BF16)

    def cumsum(g):
        hi = g.astype(BF16)
        lo = (g - hi.astype(F32)).astype(BF16)
        split = jnp.concatenate([hi, lo], axis=1)
        parts = []
        for c in range(cpb):
            both = jnp.dot(tri, split[c * cs:(c + 1) * cs], preferred_element_type=F32)
            parts.append(both[:, :LANES] + both[:, LANES:])
        return jnp.concatenate(parts, axis=0)

    def prepare(i, carry):
        rows = pl.ds(pl.multiple_of(i * prep, prep), prep)
        g = g_ref[rows, :]
        kk = kk_ref[rows, :].astype(F32)
        a = cumsum(g)
        a_end = jnp.sum(g.reshape(cpb, cs, LANES), axis=1)
        ea = jnp.exp(a)
        qt_ref[rows, :] = (q_ref[rows, :] * ea).astype(BF16)
        kt_ref[rows, :] = (kk * (1.0 / ea)).astype(BF16)
        to_end = jnp.exp(a_end[:, None, :] - a.reshape(cpb, cs, LANES)).reshape(prep, LANES)
        ke_ref[rows, :] = (kk * to_end).astype(BF16)
        dec_ref[pl.ds(pl.multiple_of(i * cpb, cpb), cpb), :] = jnp.exp(a_end)
        return carry

    lax.fori_loop(0, seq // prep, prepare, 0)

    def step(i, st):
        chunks = [i * HGRN_UNROLL + u for u in range(HGRN_UNROLL)]
        if reverse:
            chunks = [n_chunks - 1 - j for j in chunks]
        rows = [pl.ds(pl.multiple_of(c * cs, cs), cs) for c in chunks]
        upds = [jnp.dot(vt_ref[pl.ds(pl.multiple_of(c * LANES, LANES), LANES), :], ke_ref[r, :],
                        preferred_element_type=F32) for c, r in zip(chunks, rows)]
        atts = [lax.dot_general(_split_heads(qt_ref[r, :], head0), kt_ref[r, :],
                                (((1,), (1,)), ((), ())), preferred_element_type=F32)
                for r in rows]
        states = []
        for c, upd in zip(chunks, upds):
            states.append(st.astype(BF16))
            st = jnp.where(same_head, st * dec_ref[pl.ds(c, 1), :] + upd, 0.0)
        for r, att, st_in in zip(rows, atts, states):
            att = jnp.where(keep, att, 0.0).astype(BF16)
            both = jnp.dot(att, vb_ref[r, :], preferred_element_type=F32)
            o = (jnp.where(head0, both[0:cs], both[cs:])
                 + lax.dot_general(qt_ref[r, :], st_in, (((1,), (1,)), ((), ())),
                                   preferred_element_type=F32))
            if reverse:
                oacc_ref[r, :] += o
            else:
                oacc_ref[r, :] = o
        return st

    lax.fori_loop(0, n_chunks // HGRN_UNROLL, step, jnp.zeros((LANES, LANES), F32))


def _hgrn_safe(q_ref, g_ref, kk_ref, vb_ref, oacc_ref, *, seq, reverse):
    cs = HGRN_CHUNK
    n_chunks = seq // cs
    head0 = lax.broadcasted_iota(jnp.int32, (1, LANES), 1) < HEAD_DIM
    same_head = _head_block_ones()
    keep = _causal_keep(reverse)
    col = lax.broadcasted_iota(jnp.int32, (LANES, LANES), 1)
    last = 0 if reverse else cs - 1

    def chunk(j, st):
        c = n_chunks - 1 - j if reverse else j
        rows = pl.ds(pl.multiple_of(c * cs, cs), cs)
        g = g_ref[rows, :]
        kk = kk_ref[rows, :].astype(F32)
        q = q_ref[rows, :]
        vb = vb_ref[rows, :]
        a = _chunk_cumsum(g, reverse)
        a_end = a[last:last + 1, :]
        qt = (q * jnp.exp(a)).astype(BF16)

        def column(s, att):
            sel = lax.broadcasted_iota(jnp.int32, (cs, 1), 0) == s
            a_s = jnp.sum(jnp.where(sel, a, 0.0), axis=0, keepdims=True)
            kk_s = jnp.sum(jnp.where(sel, kk, 0.0), axis=0, keepdims=True)
            e = jnp.exp(jnp.minimum(a - a_s, 0.0)) * (q * kk_s)
            place = jnp.logical_and(same_head, col % HEAD_DIM == s).astype(F32)
            return att + jnp.dot(e, place, preferred_element_type=F32,
                                 precision=lax.Precision.HIGHEST)

        att = lax.fori_loop(0, cs, column, jnp.zeros((cs, LANES), F32))
        att = jnp.where(keep, att, 0.0).astype(BF16)
        o = (jnp.dot(att, _split_heads(vb, head0), preferred_element_type=F32)
             + lax.dot_general(qt, st.astype(BF16), (((1,), (1,)), ((), ())),
                               preferred_element_type=F32))
        k_end = (kk * jnp.exp(a_end - a)).astype(BF16)
        upd = lax.dot_general(vb, k_end, (((0,), (0,)), ((), ())), preferred_element_type=F32)
        st = jnp.where(same_head, st * jnp.exp(a_end) + upd, 0.0)
        if reverse:
            oacc_ref[rows, :] += o
        else:
            oacc_ref[rows, :] = o
        return st

    lax.fori_loop(0, n_chunks, chunk, jnp.zeros((LANES, LANES), F32))


def _hgrn_body(q_ref, gf_ref, gb_ref, kf_ref, kb_ref, vb_ref, gate_ref, lbf_ref, lbb_ref, nw_ref,
               out_ref, oacc_ref, vt_ref, qt_ref, kt_ref, ke_ref, dec_ref, *, seq):
    rows_n = HGRN_PREP_ROWS if seq % HGRN_PREP_ROWS == 0 else LANES

    def transpose_v(i, carry):
        for t in range(rows_n // HGRN_CHUNK):
            src = pl.ds(pl.multiple_of(i * rows_n + t * HGRN_CHUNK, HGRN_CHUNK), HGRN_CHUNK)
            dst = pl.ds(pl.multiple_of(2 * i * rows_n + t * LANES, LANES), LANES)
            vt_ref[dst, :] = vb_ref[src, :].astype(F32).T.astype(BF16)
        return carry

    lax.fori_loop(0, seq // rows_n, transpose_v, 0)
    for g_ref, kk_ref, lbp_ref, reverse in ((gf_ref, kf_ref, lbf_ref, False),
                                            (gb_ref, kb_ref, lbb_ref, True)):
        worst = jnp.max(-jnp.log(_lower_bound(lbp_ref))) * HGRN_CHUNK

        @pl.when(worst <= F32_EXP_SAFE)
        def _():
            _hgrn_fast(q_ref, g_ref, kk_ref, vb_ref, vt_ref, oacc_ref, qt_ref, kt_ref, ke_ref,
                       dec_ref, seq=seq, reverse=reverse)

        @pl.when(jnp.logical_not(worst <= F32_EXP_SAFE))
        def _():
            _hgrn_safe(q_ref, g_ref, kk_ref, vb_ref, oacc_ref, seq=seq, reverse=reverse)

    ones_blk = _head_block_ones().astype(BF16)

    def finish(i, carry):
        rows = pl.ds(pl.multiple_of(i * rows_n, rows_n), rows_n)
        o = oacc_ref[rows, :]
        sq = o * o
        hi = sq.astype(BF16)
        lo = (sq - hi.astype(F32)).astype(BF16)
        ms = (jnp.dot(hi, ones_blk, preferred_element_type=F32)
              + jnp.dot(lo, ones_blk, preferred_element_type=F32)) * (1.0 / HEAD_DIM)
        y = o * lax.rsqrt(ms + NORM_EPS) * nw_ref[...] * gate_ref[rows, :].astype(F32)
        out_ref[rows, :] = y.astype(BF16)
        return carry

    lax.fori_loop(0, seq // rows_n, finish, 0)


def _hgrn(hf, hb, lb_fwd, lb_bwd, out_norm_w, batch, seq):
    nslots = lb_fwd.shape[0]
    cols = HGRN_WIDTH // LANES

    def slab(k):
        return pl.BlockSpec((None, None, seq, LANES), lambda b, p, k=k: (b, k * cols + p, 0, 0))

    return pl.pallas_call(
        functools.partial(_hgrn_body, seq=seq),
        grid=(batch, cols),
        in_specs=[slab(0), slab(1), slab(2), slab(0), slab(1), slab(2), slab(3),
                  pl.BlockSpec((nslots, LANES), lambda b, p: (0, p)),
                  pl.BlockSpec((nslots, LANES), lambda b, p: (0, p)),
                  pl.BlockSpec((1, LANES), lambda b, p: (0, 0))],
        out_specs=slab(0),
        out_shape=jax.ShapeDtypeStruct((batch, cols, seq, LANES), BF16),
        scratch_shapes=[pltpu.VMEM((seq, LANES), F32),
                        pltpu.VMEM((seq // HGRN_CHUNK * LANES, HGRN_CHUNK), BF16),
                        pltpu.VMEM((seq, LANES), BF16),
                        pltpu.VMEM((seq, LANES), BF16),
                        pltpu.VMEM((seq, LANES), BF16),
                        pltpu.VMEM((seq // HGRN_CHUNK, LANES), F32)],
        compiler_params=_cparams(("arbitrary", "arbitrary")),
        name="hgrn",
    )(hf, hf, hf, hb, hb, hb, hb, lb_fwd, lb_bwd, out_norm_w)


FF_CHUNK = 256
HALO = 16


def _mix_ffn_body(ap_ref, a_ref, an_ref, gp_ref, g_ref, gn_ref, xp_ref, x_ref, xn_ref,
                  wo_ref, nw_mix_ref, nw_pre_ref, wg_ref, wu_ref, cw_ref, cb_ref, wd_ref,
                  nw_post_ref, y_ref, act_ref, *, tm, tiles_per_seq):
    i = pl.program_id(0)
    ext = tm + 2 * HALO

    def mixer_rows(attn_ref, hgrn_ref):
        return jnp.concatenate([attn_ref[j] for j in range(attn_ref.shape[0])]
                               + [hgrn_ref[j] for j in range(hgrn_ref.shape[0])], axis=1)

    mixed = jnp.concatenate([mixer_rows(ap_ref, gp_ref), mixer_rows(a_ref, g_ref),
                             mixer_rows(an_ref, gn_ref)], axis=0)
    mix = jnp.dot(mixed, wo_ref[...], preferred_element_type=F32)
    x_ext = jnp.concatenate([xp_ref[...], x_ref[...], xn_ref[...]], axis=0)
    h_ext = x_ext + mix * _rms_scale(mix) * nw_mix_ref[...]
    h = h_ext[HALO:HALO + tm]

    row = lax.broadcasted_iota(jnp.int32, (ext, 1), 0)
    has_prev = (i % tiles_per_seq != 0).astype(F32)
    has_next = (i % tiles_per_seq != tiles_per_seq - 1).astype(F32)
    live = jnp.where(row < HALO, has_prev, jnp.where(row >= HALO + tm, has_next, 1.0))
    xn = h_ext * _rms_scale(h_ext) * nw_pre_ref[...] * live
    xn_ext = xn.astype(BF16)
    xn_main = xn[HALO:HALO + tm].astype(BF16)
    for c in range(D_FF // FF_CHUNK):
        cols = slice(c * FF_CHUNK, (c + 1) * FF_CHUNK)
        a = jnp.dot(xn_ext, wg_ref[:, cols], preferred_element_type=F32)
        a_prev = pltpu.roll(a, 1, 0)[HALO:HALO + tm]
        a_next = pltpu.roll(a, ext - 1, 0)[HALO:HALO + tm]
        a_mid = a[HALO:HALO + tm]
        conv = (a_prev * cw_ref[0:1, cols] + a_mid * cw_ref[1:2, cols]
                + a_next * cw_ref[2:3, cols] + cb_ref[:, cols])
        b = jnp.dot(xn_main, wu_ref[:, cols], preferred_element_type=F32)
        gelu = 0.5 * conv * (1.0 + jnp.tanh(math.sqrt(2.0 / math.pi)
                                            * (conv + 0.044715 * (conv * conv * conv))))
        act_ref[:, cols] = (gelu * b).astype(BF16)
    ffn = jnp.dot(act_ref[...], wd_ref[...], preferred_element_type=F32)
    y_ref[...] = h + ffn * _rms_scale(ffn) * nw_post_ref[...]


def _mix_ffn(attn, hg, x2d, p, seq, tm):
    tokens = x2d.shape[0]
    tiles_per_seq = seq // tm
    hb = tm // HALO
    halos_per_seq = seq // HALO
    n_halo = tokens // HALO
    resident = dict(pipeline_mode=pl.Buffered(1))

    def slabs(rows, row_block):
        return pl.BlockSpec((None, HEAD_PAIRS, rows, LANES),
                            lambda i: (i // tiles_per_seq, 0, row_block(i % tiles_per_seq), 0))

    def prev_block(t):
        return jnp.maximum(t * hb - 1, 0)

    def next_block(t):
        return jnp.minimum((t + 1) * hb, halos_per_seq - 1)

    head_group = [slabs(HALO, prev_block), slabs(tm, lambda t: t), slabs(HALO, next_block)]
    row_vec = lambda n: pl.BlockSpec((1, n), lambda i: (0, 0))
    return pl.pallas_call(
        functools.partial(_mix_ffn_body, tm=tm, tiles_per_seq=tiles_per_seq),
        grid=(tokens // tm,),
        in_specs=head_group + head_group + [
            pl.BlockSpec((HALO, D_MODEL), lambda i: (jnp.maximum(i * hb - 1, 0), 0)),
            pl.BlockSpec((tm, D_MODEL), lambda i: (i, 0)),
            pl.BlockSpec((HALO, D_MODEL), lambda i: (jnp.minimum((i + 1) * hb, n_halo - 1), 0)),
            pl.BlockSpec((D_MODEL, D_MODEL), lambda i: (0, 0), **resident),
            row_vec(D_MODEL),
            row_vec(D_MODEL),
            pl.BlockSpec((D_MODEL, D_FF), lambda i: (0, 0), **resident),
            pl.BlockSpec((D_MODEL, D_FF), lambda i: (0, 0), **resident),
            pl.BlockSpec((3, D_FF), lambda i: (0, 0)),
            row_vec(D_FF),
            pl.BlockSpec((D_FF, D_MODEL), lambda i: (0, 0), **resident),
            row_vec(D_MODEL),
        ],
        out_specs=pl.BlockSpec((tm, D_MODEL), lambda i: (i, 0)),
        out_shape=jax.ShapeDtypeStruct((tokens, D_MODEL), F32),
        scratch_shapes=[pltpu.VMEM((tm, D_FF), BF16)],
        compiler_params=_cparams(("arbitrary",)),
        name="mix_ffn",
    )(attn, attn, attn, hg, hg, hg, x2d, x2d, x2d, p["w_out"], p["norm_mix_post"],
      p["norm_ffn_pre"], p["w_gate"], p["w_up"], p["conv_w"], p["conv_b"], p["w_down"],
      p["norm_ffn_post"])


def _token_tile(seq):
    return 512 if seq % 512 == 0 else seq


def _encode(x, p):
    batch, seq, _ = x.shape
    tm = _token_tile(seq)
    x2d = x.reshape(batch * seq, D_MODEL)
    qkv_orders, hf, hb = _inproj(x2d, p["norm_mix_pre"], p["w_in"], _rotary_tables(seq),
                                 p["lb_fwd"], p["lb_bwd"], batch, seq, tm)
    attn = _attention(qkv_orders, batch, seq)
    hg = _hgrn(hf, hb, p["lb_fwd"], p["lb_bwd"], p["hgrn_out_norm"], batch, seq)
    y = _mix_ffn(attn, hg, x2d, p, seq, tm)
    return y.reshape(batch, seq, D_MODEL)


def kernel(x_prompt, x_sample, norm_mix_pre, w_in, hgrn_lb_fwd, hgrn_lb_bwd, hgrn_out_norm, w_out,
           norm_mix_post, norm_ffn_pre, w_gate, w_up, conv_w, conv_b, w_down, norm_ffn_post):
    assert w_in.shape[0] == 1, "one layer"
    p = {
        "norm_mix_pre": norm_mix_pre[0][None, :],
        "w_in": w_in[0].astype(BF16),
        "lb_fwd": hgrn_lb_fwd.astype(F32),
        "lb_bwd": hgrn_lb_bwd.astype(F32),
        "hgrn_out_norm": jnp.tile(hgrn_out_norm[0], LANES // HEAD_DIM)[None, :],
        "w_out": w_out[0].astype(BF16),
        "norm_mix_post": norm_mix_post[0][None, :],
        "norm_ffn_pre": norm_ffn_pre[0][None, :],
        "w_gate": w_gate[0].astype(BF16),
        "w_up": w_up[0].astype(BF16),
        "conv_w": conv_w[0],
        "conv_b": conv_b[0][None, :],
        "w_down": w_down[0].astype(BF16),
        "norm_ffn_post": norm_ffn_post[0][None, :],
    }
    return (_encode(x_prompt, p), _encode(x_sample, p))
```

```python
import functools
import math

import jax
import jax.numpy as jnp
from jax import lax
from jax.experimental import pallas as pl
from jax.experimental.pallas import tpu as pltpu

F32 = jnp.float32
BF16 = jnp.bfloat16

D_MODEL = 1024
HEAD_DIM = 64
ATTN_HEADS = 8
HGRN_HEADS = 8
ATTN_WIDTH = ATTN_HEADS * HEAD_DIM
HGRN_WIDTH = HGRN_HEADS * HEAD_DIM
QKV_WIDTH = 3 * ATTN_WIDTH
HPROJ_WIDTH = 5 * HGRN_WIDTH
IN_WIDTH = QKV_WIDTH + HPROJ_WIDTH
DILATED_BRANCHES = ((128, 1), (512, 4), (2048, 16))
ROPE_THETA = 500000.0
ROT_DIM = HEAD_DIM // 4
ROT_HALF = ROT_DIM // 2
HGRN_CHUNK = 64
D_FF = 2816
NORM_EPS = 1e-6
NEG_INF = -1e30

LANES = 128
SUBLANES = 8
HEAD_PAIRS = ATTN_WIDTH // LANES
VMEM_LIMIT_BYTES = 56 * 1024 * 1024
F32_EXP_SAFE = 80.0


def _cparams(sem):
    return pltpu.CompilerParams(dimension_semantics=sem, vmem_limit_bytes=VMEM_LIMIT_BYTES)


def _rms_scale(x):
    return lax.rsqrt(jnp.mean(x * x, axis=-1, keepdims=True) + NORM_EPS)


def _sigmoid(z):
    return 1.0 / (1.0 + jnp.exp(-z))


def _head_block_ones():
    r = lax.broadcasted_iota(jnp.int32, (LANES, LANES), 0) // HEAD_DIM
    c = lax.broadcasted_iota(jnp.int32, (LANES, LANES), 1) // HEAD_DIM
    return r == c


IN_CHUNK = 256
IN_ROWS = 128
Q_SCALE = math.log2(math.e) / math.sqrt(HEAD_DIM)


def _lower_bound(lbp_ref):
    p = lbp_ref[...]
    e = jnp.exp(p - jnp.max(p, axis=0, keepdims=True))
    return e[0:1, :] / jnp.sum(e, axis=0, keepdims=True)


def _inproj_body(x_ref, nw_ref, w_ref, rot_ref, lbf_ref, lbb_ref, *refs, dils):
    qkv_ref = refs[0]
    perm_refs = refs[1:len(dils)]
    hf_ref, hb_ref, tile_a, tile_b = refs[len(dils):]
    tm = x_ref.shape[0]
    x = x_ref[...]
    xn = (x * _rms_scale(x) * nw_ref[...]).astype(BF16)
    lbs = (_lower_bound(lbf_ref), _lower_bound(lbb_ref))
    n_pairs = HGRN_WIDTH // LANES
    n_chunks = IN_WIDTH // IN_CHUNK
    i_chunk = (QKV_WIDTH + 3 * HGRN_WIDTH) // IN_CHUNK
    order = [c for c in range(n_chunks) if c * IN_CHUNK >= QKV_WIDTH and c != i_chunk]
    order += [c for c in range(n_chunks) if c * IN_CHUNK < QKV_WIDTH] + [i_chunk]
    for c in order:
        acc = jnp.dot(xn, w_ref[:, c * IN_CHUNK:(c + 1) * IN_CHUNK], preferred_element_type=F32)
        for g in range(IN_CHUNK // LANES):
            lo = c * IN_CHUNK + g * LANES
            slab = lo // LANES
            group, pair = divmod((lo - QKV_WIDTH) // LANES, n_pairs)
            for rb in range(tm // IN_ROWS):
                rs = slice(rb * IN_ROWS, (rb + 1) * IN_ROWS)
                a = acc[rs, g * LANES:(g + 1) * LANES]
                if lo < QKV_WIDTH:
                    if lo < 2 * ATTN_WIDTH:
                        a = (a * rot_ref[0, rs, :] + pltpu.roll(a, LANES - ROT_HALF, 1) * rot_ref[1, rs, :]
                             + pltpu.roll(a, ROT_HALF, 1) * rot_ref[2, rs, :])
                        if lo < ATTN_WIDTH:
                            a = a * Q_SCALE
                    qkv_ref[slab, rs, :] = a.astype(BF16)
                    tile_a[rs, :] = a
                elif group == 0:
                    hf_ref[pair, rs, :] = a * _sigmoid(a)
                elif group == 4:
                    hb_ref[3 * n_pairs + pair, rs, :] = (a * _sigmoid(a)).astype(BF16)
                elif group == 3:
                    hb_ref[2 * n_pairs + pair, rs, :] = a.astype(BF16)
                else:
                    lb = lbs[group - 1][:, pair * LANES:(pair + 1) * LANES]
                    sig = _sigmoid(a)
                    hf_ref[group * n_pairs + pair, rs, :] = jnp.log(lb + (1.0 - lb) * sig)
                    hb_ref[(group - 1) * n_pairs + pair, rs, :] = (
                        (1.0 - lb) * (1.0 - sig)).astype(BF16)
            if lo < QKV_WIDTH:
                tiles = (tile_a, tile_b)
                for li in range(1, len(dils)):
                    d_prev, dil = dils[li - 1], dils[li]
                    ratio, len_prev, sub = dil // d_prev, tm // d_prev, tm // dil
                    t_in, t_out = tiles[(li - 1) % 2], tiles[li % 2]
                    for seg in range(d_prev):
                        for k in range(ratio):
                            rows = t_in[pl.ds(seg * len_prev + k, sub, stride=ratio), :]
                            res = seg + d_prev * k
                            perm_refs[li - 1][slab, res] = rows.astype(BF16)
                            if li + 1 < len(dils):
                                t_out[res * sub:(res + 1) * sub, :] = rows


def _rotary_tables(seq):
    inv_freq = ROPE_THETA ** (-jnp.arange(ROT_HALF, dtype=F32) * 2.0 / ROT_DIM)
    ang = jnp.arange(seq).astype(F32)[:, None] * inv_freq[None, :]
    cos, sin = jnp.cos(ang), jnp.sin(ang)
    zeros = jnp.zeros((seq, HEAD_DIM - ROT_DIM), F32)
    half0 = jnp.zeros((seq, ROT_HALF), F32)
    cos_h = jnp.concatenate([cos, cos, 1.0 + zeros], axis=1)
    lo_h = jnp.concatenate([-sin, half0, zeros], axis=1)
    hi_h = jnp.concatenate([half0, sin, zeros], axis=1)
    per_head = jnp.stack([cos_h, lo_h, hi_h], axis=0)
    return jnp.concatenate([per_head] * (LANES // HEAD_DIM), axis=2)


def _inproj(x2d, norm_w, w_in_bf16, rot, lb_fwd, lb_bwd, batch, seq, tm):
    tiles_per_seq = seq // tm
    n_qkv = QKV_WIDTH // LANES
    n_pairs = HGRN_WIDTH // LANES
    nslots = lb_fwd.shape[0]
    dils = tuple(d for _, d in DILATED_BRANCHES)
    assert dils[0] == 1 and all(b % a == 0 for a, b in zip(dils, dils[1:]))

    def slabs(n):
        return pl.BlockSpec((None, n, tm, LANES),
                            lambda i: (i // tiles_per_seq, 0, i % tiles_per_seq, 0))

    def residues(d):
        return pl.BlockSpec((None, n_qkv, d, tm // d, LANES),
                            lambda i: (i // tiles_per_seq, 0, 0, i % tiles_per_seq, 0))

    outs = pl.pallas_call(
        functools.partial(_inproj_body, dils=dils),
        grid=(batch * tiles_per_seq,),
        in_specs=[
            pl.BlockSpec((tm, D_MODEL), lambda i: (i, 0)),
            pl.BlockSpec((1, D_MODEL), lambda i: (0, 0)),
            pl.BlockSpec((D_MODEL, IN_WIDTH), lambda i: (0, 0), pipeline_mode=pl.Buffered(1)),
            pl.BlockSpec((3, tm, LANES), lambda i: (0, i % tiles_per_seq, 0)),
            pl.BlockSpec((nslots, HGRN_WIDTH), lambda i: (0, 0)),
            pl.BlockSpec((nslots, HGRN_WIDTH), lambda i: (0, 0)),
        ],
        out_specs=([slabs(n_qkv)] + [residues(d) for d in dils[1:]]
                   + [slabs(3 * n_pairs), slabs(4 * n_pairs)]),
        out_shape=([jax.ShapeDtypeStruct((batch, n_qkv, seq, LANES), BF16)]
                   + [jax.ShapeDtypeStruct((batch, n_qkv, d, seq // d, LANES), BF16)
                      for d in dils[1:]]
                   + [jax.ShapeDtypeStruct((batch, 3 * n_pairs, seq, LANES), F32),
                      jax.ShapeDtypeStruct((batch, 4 * n_pairs, seq, LANES), BF16)]),
        scratch_shapes=[pltpu.VMEM((tm, LANES), F32), pltpu.VMEM((tm, LANES), F32)],
        compiler_params=_cparams(("arbitrary",)),
        name="inproj",
    )(x2d, norm_w, w_in_bf16, rot, lb_fwd, lb_bwd)
    qkv_orders = [outs[0]] + [o.reshape(batch, n_qkv, seq, LANES) for o in outs[1:len(dils)]]
    return qkv_orders, outs[len(dils)], outs[len(dils) + 1]


ATTN_TQ = 128
ATTN_GROUP = 8
MERGE_ROWS = 1024


def _attn_body(*refs, seq):
    nb = len(DILATED_BRANCHES)
    qkv_refs = [refs[3 * bi:3 * bi + 3] for bi in range(nb)]
    out_ref, bias_ref, o_ref, lse_ref = refs[3 * nb:]
    lane = lax.broadcasted_iota(jnp.int32, (1, LANES), 1)
    head0 = lane < HEAD_DIM

    for bi, (window, dil) in enumerate(DILATED_BRANCHES):
        sub_len = seq // dil
        n_side = window // (2 * dil)
        tq = min(ATTN_TQ, sub_len)
        ks_len = min(sub_len, tq + 2 * n_side)
        nqb = sub_len // tq
        srcs = qkv_refs[bi]

        assert tq == 2 * n_side or nqb == 1
        heads = LANES // HEAD_DIM
        @pl.when(jnp.logical_and(pl.program_id(0) == 0, pl.program_id(1) == 0))
        def _(bi=bi, n_side=n_side, tq=tq, rows_b=heads * tq, ks_len=ks_len):
            row = lax.broadcasted_iota(jnp.int32, (rows_b, ks_len), 0) % tq
            col = lax.broadcasted_iota(jnp.int32, (rows_b, ks_len), 1)
            for di in range(3):
                inside = jnp.abs(row - col + di * n_side) <= n_side
                bias_ref[bi, di, 0:rows_b, 0:ks_len] = jnp.where(inside, 0.0, NEG_INF)

        def q_group(it, carry, bi=bi, dil=dil, sub_len=sub_len, n_side=n_side, tq=tq,
                    ks_len=ks_len, nqb=nqb, srcs=srcs, heads=heads):
            work = []
            for n in range(ATTN_GROUP):
                f = it * ATTN_GROUP + n
                r = f // nqb
                q0 = (f % nqb) * tq
                base = r * sub_len
                ks = jnp.clip(q0 - n_side, 0, sub_len - ks_len)
                q = srcs[0][pl.ds(pl.multiple_of(base + q0, tq), tq), :]
                k = srcs[1][pl.ds(pl.multiple_of(base + ks, n_side), ks_len), :]
                v = srcs[2][pl.ds(pl.multiple_of(base + ks, n_side), ks_len), :]
                zero = jnp.zeros_like(q)
                qq = jnp.concatenate([jnp.where(head0, q, zero), jnp.where(head0, zero, q)], axis=0)
                s = lax.dot_general(qq, k, (((1,), (1,)), ((), ())), preferred_element_type=F32)
                s = s + bias_ref[bi, (q0 - ks) // n_side, 0:heads * tq, 0:ks_len]
                if dil == 1:
                    rows = pl.ds(pl.multiple_of(q0, tq), tq)
                else:
                    rows = pl.ds(r + q0 * dil, tq, stride=dil)
                work.append((s, v, rows))
            for s, v, rows in work:
                m = jnp.max(s, axis=-1, keepdims=True)
                p = jnp.exp2(s - m)
                l = jnp.sum(p, axis=-1, keepdims=True)
                pv = jnp.dot(p.astype(BF16), v, preferred_element_type=F32)
                l = jnp.where(head0, l[0:tq], l[tq:])
                o_ref.at[bi][rows, :] = jnp.where(head0, pv[0:tq], pv[tq:]) * (1.0 / l)
                lse_ref.at[bi][rows, :] = jnp.where(head0, m[0:tq], m[tq:]) + jnp.log2(l)
            return carry

        lax.fori_loop(0, dil * nqb // ATTN_GROUP, q_group, 0)

    rows_m = min(MERGE_ROWS, seq)

    def merge(i, carry):
        rows = pl.ds(pl.multiple_of(i * rows_m, rows_m), rows_m)
        l1, l2, l3 = lse_ref.at[0][rows, :], lse_ref.at[1][rows, :], lse_ref.at[2][rows, :]
        m = jnp.maximum(jnp.maximum(l1, l2), l3)
        e1, e2, e3 = jnp.exp2(l1 - m), jnp.exp2(l2 - m), jnp.exp2(l3 - m)
        num = e1 * o_ref.at[0][rows, :] + e2 * o_ref.at[1][rows, :] + e3 * o_ref.at[2][rows, :]
        out_ref[rows, :] = (num * (1.0 / (e1 + e2 + e3))).astype(BF16)
        return carry

    lax.fori_loop(0, seq // rows_m, merge, 0)


def _attention(qkv_orders, batch, seq):
    nb = len(DILATED_BRANCHES)
    n_side = max(w // (2 * d) for w, d in DILATED_BRANCHES)

    def slab(k):
        return pl.BlockSpec((None, None, seq, LANES),
                            lambda b, p, k=k: (b, k * HEAD_PAIRS + p, 0, 0))

    operands = [arr for arr in qkv_orders for _ in range(3)]
    return pl.pallas_call(
        functools.partial(_attn_body, seq=seq),
        grid=(batch, HEAD_PAIRS),
        in_specs=[slab(k) for _ in qkv_orders for k in range(3)],
        out_specs=slab(0),
        out_shape=jax.ShapeDtypeStruct((batch, HEAD_PAIRS, seq, LANES), BF16),
        scratch_shapes=[pltpu.VMEM((nb, 3, LANES // HEAD_DIM * ATTN_TQ, ATTN_TQ + 2 * n_side), F32),
                        pltpu.VMEM((nb, seq, LANES), F32),
                        pltpu.VMEM((nb, seq, LANES), F32)],
        compiler_params=_cparams(("arbitrary", "arbitrary")),
        name="attention",
    )(*operands)


HGRN_UNROLL = 32
HGRN_PREP_ROWS = 1024


def _chunk_cumsum(g, reverse):
    rows = g.shape[0]
    pos = lax.broadcasted_iota(jnp.int32, (rows, 1), 0) % HGRN_CHUNK
    sh = 1
    while sh < HGRN_CHUNK:
        if reverse:
            shifted = pltpu.roll(g, rows - sh, 0)
            valid = pos < HGRN_CHUNK - sh
        else:
            shifted = pltpu.roll(g, sh, 0)
            valid = pos >= sh
        g = g + jnp.where(valid, shifted, 0.0)
        sh *= 2
    return g


def _split_heads(x, head0):
    zero = jnp.zeros_like(x)
    return jnp.concatenate([jnp.where(head0, x, zero), jnp.where(head0, zero, x)], axis=0)


def _causal_keep(reverse):
    t_l = lax.broadcasted_iota(jnp.int32, (HGRN_CHUNK, LANES), 0)
    s_l = lax.broadcasted_iota(jnp.int32, (HGRN_CHUNK, LANES), 1) % HEAD_DIM
    return (s_l >= t_l) if reverse else (s_l <= t_l)


def _hgrn_fast(q_ref, g_ref, kk_ref, vb_ref, vt_ref, oacc_ref, qt_ref, kt_ref, ke_ref, dec_ref,
               *, seq, reverse):
    cs = HGRN_CHUNK
    n_chunks = seq // cs
    prep = HGRN_PREP_ROWS if seq % HGRN_PREP_ROWS == 0 else cs * SUBLANES
    cpb = prep // cs
    head0 = lax.broadcasted_iota(jnp.int32, (1, LANES), 1) < HEAD_DIM
    same_head = _head_block_ones()
    t_h = lax.broadcasted_iota(jnp.int32, (2 * cs, cs), 0) % cs
    s_h = lax.broadcasted_iota(jnp.int32, (2 * cs, cs), 1)
    keep = (s_h >= t_h) if reverse else (s_h <= t_h)

    t_i = lax.broadcasted_iota(jnp.int32, (cs, cs), 0)
    s_i = lax.broadcasted_iota(jnp.int32, (cs, cs), 1)
    tri = ((s_i >= t_i) if reverse else (s_i <= t_i)).astype(BF16)

    def cumsum(g):
        hi = g.astype(BF16)
        lo = (g - hi.astype(F32)).astype(BF16)
        split = jnp.concatenate([hi, lo], axis=1)
        parts = []
        for c in range(cpb):
            both = jnp.dot(tri, split[c * cs:(c + 1) * cs], preferred_element_type=F32)
            parts.append(both[:, :LANES] + both[:, LANES:])
        return jnp.concatenate(parts, axis=0)

    def prepare(i, carry):
        rows = pl.ds(pl.multiple_of(i * prep, prep), prep)
        g = g_ref[rows, :]
        kk = kk_ref[rows, :].astype(F32)
        a = cumsum(g)
        a_end = jnp.sum(g.reshape(cpb, cs, LANES), axis=1)
        ea = jnp.exp(a)
        qt_ref[rows, :] = (q_ref[rows, :] * ea).astype(BF16)
        kt_ref[rows, :] = (kk * (1.0 / ea)).astype(BF16)
        to_end = jnp.exp(a_end[:, None, :] - a.reshape(cpb, cs, LANES)).reshape(prep, LANES)
        ke_ref[rows, :] = (kk * to_end).astype(BF16)
        dec_ref[pl.ds(pl.multiple_of(i * cpb, cpb), cpb), :] = jnp.exp(a_end)
        return carry

    lax.fori_loop(0, seq // prep, prepare, 0)

    def step(i, st):
        chunks = [i * HGRN_UNROLL + u for u in range(HGRN_UNROLL)]
        if reverse:
            chunks = [n_chunks - 1 - j for j in chunks]
        rows = [pl.ds(pl.multiple_of(c * cs, cs), cs) for c in chunks]
        upds = [jnp.dot(vt_ref[pl.ds(pl.multiple_of(c * LANES, LANES), LANES), :], ke_ref[r, :],
                        preferred_element_type=F32) for c, r in zip(chunks, rows)]
        atts = [lax.dot_general(_split_heads(qt_ref[r, :], head0), kt_ref[r, :],
                                (((1,), (1,)), ((), ())), preferred_element_type=F32)
                for r in rows]
        states = []
        for c, upd in zip(chunks, upds):
            states.append(st.astype(BF16))
            st = jnp.where(same_head, st * dec_ref[pl.ds(c, 1), :] + upd, 0.0)
        for r, att, st_in in zip(rows, atts, states):
            att = jnp.where(keep, att, 0.0).astype(BF16)
            both = jnp.dot(att, vb_ref[r, :], preferred_element_type=F32)
            o = (jnp.where(head0, both[0:cs], both[cs:])
                 + lax.dot_general(qt_ref[r, :], st_in, (((1,), (1,)), ((), ())),
                                   preferred_element_type=F32))
            if reverse:
                oacc_ref[r, :] += o
            else:
                oacc_ref[r, :] = o
        return st

    lax.fori_loop(0, n_chunks // HGRN_UNROLL, step, jnp.zeros((LANES, LANES), F32))


def _hgrn_safe(q_ref, g_ref, kk_ref, vb_ref, oacc_ref, *, seq, reverse):
    cs = HGRN_CHUNK
    n_chunks = seq // cs
    head0 = lax.broadcasted_iota(jnp.int32, (1, LANES), 1) < HEAD_DIM
    same_head = _head_block_ones()
    keep = _causal_keep(reverse)
    col = lax.broadcasted_iota(jnp.int32, (LANES, LANES), 1)
    last = 0 if reverse else cs - 1

    def chunk(j, st):
        c = n_chunks - 1 - j if reverse else j
        rows = pl.ds(pl.multiple_of(c * cs, cs), cs)
        g = g_ref[rows, :]
        kk = kk_ref[rows, :].astype(F32)
        q = q_ref[rows, :]
        vb = vb_ref[rows, :]
        a = _chunk_cumsum(g, reverse)
        a_end = a[last:last + 1, :]
        qt = (q * jnp.exp(a)).astype(BF16)

        def column(s, att):
            sel = lax.broadcasted_iota(jnp.int32, (cs, 1), 0) == s
            a_s = jnp.sum(jnp.where(sel, a, 0.0), axis=0, keepdims=True)
            kk_s = jnp.sum(jnp.where(sel, kk, 0.0), axis=0, keepdims=True)
            e = jnp.exp(jnp.minimum(a - a_s, 0.0)) * (q * kk_s)
            place = jnp.logical_and(same_head, col % HEAD_DIM == s).astype(F32)
            return att + jnp.dot(e, place, preferred_element_type=F32,
                                 precision=lax.Precision.HIGHEST)

        att = lax.fori_loop(0, cs, column, jnp.zeros((cs, LANES), F32))
        att = jnp.where(keep, att, 0.0).astype(BF16)
        o = (jnp.dot(att, _split_heads(vb, head0), preferred_element_type=F32)
             + lax.dot_general(qt, st.astype(BF16), (((1,), (1,)), ((), ())),
                               preferred_element_type=F32))
        k_end = (kk * jnp.exp(a_end - a)).astype(BF16)
        upd = lax.dot_general(vb, k_end, (((0,), (0,)), ((), ())), preferred_element_type=F32)
        st = jnp.where(same_head, st * jnp.exp(a_end) + upd, 0.0)
        if reverse:
            oacc_ref[rows, :] += o
        else:
            oacc_ref[rows, :] = o
        return st

    lax.fori_loop(0, n_chunks, chunk, jnp.zeros((LANES, LANES), F32))


def _hgrn_body(q_ref, gf_ref, gb_ref, kf_ref, kb_ref, vb_ref, gate_ref, lbf_ref, lbb_ref, nw_ref,
               out_ref, oacc_ref, vt_ref, qt_ref, kt_ref, ke_ref, dec_ref, *, seq):
    rows_n = HGRN_PREP_ROWS if seq % HGRN_PREP_ROWS == 0 else LANES

    def transpose_v(i, carry):
        for t in range(rows_n // HGRN_CHUNK):
            src = pl.ds(pl.multiple_of(i * rows_n + t * HGRN_CHUNK, HGRN_CHUNK), HGRN_CHUNK)
            dst = pl.ds(pl.multiple_of(2 * i * rows_n + t * LANES, LANES), LANES)
            vt_ref[dst, :] = vb_ref[src, :].astype(F32).T.astype(BF16)
        return carry

    lax.fori_loop(0, seq // rows_n, transpose_v, 0)
    for g_ref, kk_ref, lbp_ref, reverse in ((gf_ref, kf_ref, lbf_ref, False),
                                            (gb_ref, kb_ref, lbb_ref, True)):
        worst = jnp.max(-jnp.log(_lower_bound(lbp_ref))) * HGRN_CHUNK

        @pl.when(worst <= F32_EXP_SAFE)
        def _():
            _hgrn_fast(q_ref, g_ref, kk_ref, vb_ref, vt_ref, oacc_ref, qt_ref, kt_ref, ke_ref,
                       dec_ref, seq=seq, reverse=reverse)

        @pl.when(jnp.logical_not(worst <= F32_EXP_SAFE))
        def _():
            _hgrn_safe(q_ref, g_ref, kk_ref, vb_ref, oacc_ref, seq=seq, reverse=reverse)

    ones_blk = _head_block_ones().astype(BF16)

    def finish(i, carry):
        rows = pl.ds(pl.multiple_of(i * rows_n, rows_n), rows_n)
        o = oacc_ref[rows, :]
        sq = o * o
        hi = sq.astype(BF16)
        lo = (sq - hi.astype(F32)).astype(BF16)
        ms = (jnp.dot(hi, ones_blk, preferred_element_type=F32)
              + jnp.dot(lo, ones_blk, preferred_element_type=F32)) * (1.0 / HEAD_DIM)
        y = o * lax.rsqrt(ms + NORM_EPS) * nw_ref[...] * gate_ref[rows, :].astype(F32)
        out_ref[rows, :] = y.astype(BF16)
        return carry

    lax.fori_loop(0, seq // rows_n, finish, 0)


def _hgrn(hf, hb, lb_fwd, lb_bwd, out_norm_w, batch, seq):
    nslots = lb_fwd.shape[0]
    cols = HGRN_WIDTH // LANES

    def slab(k):
        return pl.BlockSpec((None, None, seq, LANES), lambda b, p, k=k: (b, k * cols + p, 0, 0))

    return pl.pallas_call(
        functools.partial(_hgrn_body, seq=seq),
        grid=(batch, cols),
        in_specs=[slab(0), slab(1), slab(2), slab(0), slab(1), slab(2), slab(3),
                  pl.BlockSpec((nslots, LANES), lambda b, p: (0, p)),
                  pl.BlockSpec((nslots, LANES), lambda b, p: (0, p)),
                  pl.BlockSpec((1, LANES), lambda b, p: (0, 0))],
        out_specs=slab(0),
        out_shape=jax.ShapeDtypeStruct((batch, cols, seq, LANES), BF16),
        scratch_shapes=[pltpu.VMEM((seq, LANES), F32),
                        pltpu.VMEM((seq // HGRN_CHUNK * LANES, HGRN_CHUNK), BF16),
                        pltpu.VMEM((seq, LANES), BF16),
                        pltpu.VMEM((seq, LANES), BF16),
                        pltpu.VMEM((seq, LANES), BF16),
                        pltpu.VMEM((seq // HGRN_CHUNK, LANES), F32)],
        compiler_params=_cparams(("arbitrary", "arbitrary")),
        name="hgrn",
    )(hf, hf, hf, hb, hb, hb, hb, lb_fwd, lb_bwd, out_norm_w)


FF_CHUNK = 256
HALO = 16


def _mix_ffn_body(ap_ref, a_ref, an_ref, gp_ref, g_ref, gn_ref, xp_ref, x_ref, xn_ref,
                  wo_ref, nw_mix_ref, nw_pre_ref, wg_ref, wu_ref, cw_ref, cb_ref, wd_ref,
                  nw_post_ref, y_ref, act_ref, *, tm, tiles_per_seq):
    i = pl.program_id(0)
    ext = tm + 2 * HALO

    def mixer_rows(attn_ref, hgrn_ref):
        return jnp.concatenate([attn_ref[j] for j in range(attn_ref.shape[0])]
                               + [hgrn_ref[j] for j in range(hgrn_ref.shape[0])], axis=1)

    mixed = jnp.concatenate([mixer_rows(ap_ref, gp_ref), mixer_rows(a_ref, g_ref),
                             mixer_rows(an_ref, gn_ref)], axis=0)
    mix = jnp.dot(mixed, wo_ref[...], preferred_element_type=F32)
    x_ext = jnp.concatenate([xp_ref[...], x_ref[...], xn_ref[...]], axis=0)
    h_ext = x_ext + mix * _rms_scale(mix) * nw_mix_ref[...]
    h = h_ext[HALO:HALO + tm]

    row = lax.broadcasted_iota(jnp.int32, (ext, 1), 0)
    has_prev = (i % tiles_per_seq != 0).astype(F32)
    has_next = (i % tiles_per_seq != tiles_per_seq - 1).astype(F32)
    live = jnp.where(row < HALO, has_prev, jnp.where(row >= HALO + tm, has_next, 1.0))
    xn = h_ext * _rms_scale(h_ext) * nw_pre_ref[...] * live
    xn_ext = xn.astype(BF16)
    xn_main = xn[HALO:HALO + tm].astype(BF16)
    for c in range(D_FF // FF_CHUNK):
        cols = slice(c * FF_CHUNK, (c + 1) * FF_CHUNK)
        a = jnp.dot(xn_ext, wg_ref[:, cols], preferred_element_type=F32)
        a_prev = pltpu.roll(a, 1, 0)[HALO:HALO + tm]
        a_next = pltpu.roll(a, ext - 1, 0)[HALO:HALO + tm]
        a_mid = a[HALO:HALO + tm]
        conv = (a_prev * cw_ref[0:1, cols] + a_mid * cw_ref[1:2, cols]
                + a_next * cw_ref[2:3, cols] + cb_ref[:, cols])
        b = jnp.dot(xn_main, wu_ref[:, cols], preferred_element_type=F32)
        gelu = 0.5 * conv * (1.0 + jnp.tanh(math.sqrt(2.0 / math.pi)
                                            * (conv + 0.044715 * (conv * conv * conv))))
        act_ref[:, cols] = (gelu * b).astype(BF16)
    ffn = jnp.dot(act_ref[...], wd_ref[...], preferred_element_type=F32)
    y_ref[...] = h + ffn * _rms_scale(ffn) * nw_post_ref[...]


def _mix_ffn(attn, hg, x2d, p, seq, tm):
    tokens = x2d.shape[0]
    tiles_per_seq = seq // tm
    hb = tm // HALO
    halos_per_seq = seq // HALO
    n_halo = tokens // HALO
    resident = dict(pipeline_mode=pl.Buffered(1))

    def slabs(rows, row_block):
        return pl.BlockSpec((None, HEAD_PAIRS, rows, LANES),
                            lambda i: (i // tiles_per_seq, 0, row_block(i % tiles_per_seq), 0))

    def prev_block(t):
        return jnp.maximum(t * hb - 1, 0)

    def next_block(t):
        return jnp.minimum((t + 1) * hb, halos_per_seq - 1)

    head_group = [slabs(HALO, prev_block), slabs(tm, lambda t: t), slabs(HALO, next_block)]
    row_vec = lambda n: pl.BlockSpec((1, n), lambda i: (0, 0))
    return pl.pallas_call(
        functools.partial(_mix_ffn_body, tm=tm, tiles_per_seq=tiles_per_seq),
        grid=(tokens // tm,),
        in_specs=head_group + head_group + [
            pl.BlockSpec((HALO, D_MODEL), lambda i: (jnp.maximum(i * hb - 1, 0), 0)),
            pl.BlockSpec((tm, D_MODEL), lambda i: (i, 0)),
            pl.BlockSpec((HALO, D_MODEL), lambda i: (jnp.minimum((i + 1) * hb, n_halo - 1), 0)),
            pl.BlockSpec((D_MODEL, D_MODEL), lambda i: (0, 0), **resident),
            row_vec(D_MODEL),
            row_vec(D_MODEL),
            pl.BlockSpec((D_MODEL, D_FF), lambda i: (0, 0), **resident),
            pl.BlockSpec((D_MODEL, D_FF), lambda i: (0, 0), **resident),
            pl.BlockSpec((3, D_FF), lambda i: (0, 0)),
            row_vec(D_FF),
            pl.BlockSpec((D_FF, D_MODEL), lambda i: (0, 0), **resident),
            row_vec(D_MODEL),
        ],
        out_specs=pl.BlockSpec((tm, D_MODEL), lambda i: (i, 0)),
        out_shape=jax.ShapeDtypeStruct((tokens, D_MODEL), F32),
        scratch_shapes=[pltpu.VMEM((tm, D_FF), BF16)],
        compiler_params=_cparams(("arbitrary",)),
        name="mix_ffn",
    )(attn, attn, attn, hg, hg, hg, x2d, x2d, x2d, p["w_out"], p["norm_mix_post"],
      p["norm_ffn_pre"], p["w_gate"], p["w_up"], p["conv_w"], p["conv_b"], p["w_down"],
      p["norm_ffn_post"])


def _token_tile(seq):
    return 512 if seq % 512 == 0 else seq


def _encode(x, p):
    batch, seq, _ = x.shape
    tm = _token_tile(seq)
    x2d = x.reshape(batch * seq, D_MODEL)
    qkv_orders, hf, hb = _inproj(x2d, p["norm_mix_pre"], p["w_in"], _rotary_tables(seq),
                                 p["lb_fwd"], p["lb_bwd"], batch, seq, tm)
    attn = _attention(qkv_orders, batch, seq)
    hg = _hgrn(hf, hb, p["lb_fwd"], p["lb_bwd"], p["hgrn_out_norm"], batch, seq)
    y = _mix_ffn(attn, hg, x2d, p, seq, tm)
    return y.reshape(batch, seq, D_MODEL)


def kernel(x_prompt, x_sample, norm_mix_pre, w_in, hgrn_lb_fwd, hgrn_lb_bwd, hgrn_out_norm, w_out,
           norm_mix_post, norm_ffn_pre, w_gate, w_up, conv_w, conv_b, w_down, norm_ffn_post):
    assert w_in.shape[0] == 1, "one layer"
    p = {
        "norm_mix_pre": norm_mix_pre[0][None, :],
        "w_in": w_in[0].astype(BF16),
        "lb_fwd": hgrn_lb_fwd.astype(F32),
        "lb_bwd": hgrn_lb_bwd.astype(F32),
        "hgrn_out_norm": jnp.tile(hgrn_out_norm[0], LANES // HEAD_DIM)[None, :],
        "w_out": w_out[0].astype(BF16),
        "norm_mix_post": norm_mix_post[0][None, :],
        "norm_ffn_pre": norm_ffn_pre[0][None, :],
        "w_gate": w_gate[0].astype(BF16),
        "w_up": w_up[0].astype(BF16),
        "conv_w": conv_w[0],
        "conv_b": conv_b[0][None, :],
        "w_down": w_down[0].astype(BF16),
        "norm_ffn_post": norm_ffn_post[0][None, :],
    }
    return (_encode(x_prompt, p), _encode(x_sample, p))
```

```python
import functools
import math

import jax
import jax.numpy as jnp
from jax import lax
from jax.experimental import pallas as pl
from jax.experimental.pallas import tpu as pltpu

F32 = jnp.float32
BF16 = jnp.bfloat16

D_MODEL = 1024
HEAD_DIM = 64
ATTN_HEADS = 8
HGRN_HEADS = 8
ATTN_WIDTH = ATTN_HEADS * HEAD_DIM
HGRN_WIDTH = HGRN_HEADS * HEAD_DIM
QKV_WIDTH = 3 * ATTN_WIDTH
HPROJ_WIDTH = 5 * HGRN_WIDTH
IN_WIDTH = QKV_WIDTH + HPROJ_WIDTH
DILATED_BRANCHES = ((128, 1), (512, 4), (2048, 16))
ROPE_THETA = 500000.0
ROT_DIM = HEAD_DIM // 4
ROT_HALF = ROT_DIM // 2
HGRN_CHUNK = 64
D_FF = 2816
NORM_EPS = 1e-6
NEG_INF = -1e30

LANES = 128
SUBLANES = 8
HEAD_PAIRS = ATTN_WIDTH // LANES
VMEM_LIMIT_BYTES = 56 * 1024 * 1024
F32_EXP_SAFE = 80.0


def _cparams(sem):
    return pltpu.CompilerParams(dimension_semantics=sem, vmem_limit_bytes=VMEM_LIMIT_BYTES)


def _rms_scale(x):
    return lax.rsqrt(jnp.mean(x * x, axis=-1, keepdims=True) + NORM_EPS)


def _sigmoid(z):
    return 1.0 / (1.0 + jnp.exp(-z))


def _head_block_ones():
    r = lax.broadcasted_iota(jnp.int32, (LANES, LANES), 0) // HEAD_DIM
    c = lax.broadcasted_iota(jnp.int32, (LANES, LANES), 1) // HEAD_DIM
    return r == c


IN_CHUNK = 256
IN_ROWS = 128
Q_SCALE = math.log2(math.e) / math.sqrt(HEAD_DIM)


def _lower_bound(lbp_ref):
    p = lbp_ref[...]
    e = jnp.exp(p - jnp.max(p, axis=0, keepdims=True))
    return e[0:1, :] / jnp.sum(e, axis=0, keepdims=True)


def _inproj_body(x_ref, nw_ref, w_ref, rot_ref, lbf_ref, lbb_ref, *refs, dils):
    qkv_ref = refs[0]
    perm_refs = refs[1:len(dils)]
    hf_ref, hb_ref, tile_a, tile_b = refs[len(dils):]
    tm = x_ref.shape[0]
    x = x_ref[...]
    xn = (x * _rms_scale(x) * nw_ref[...]).astype(BF16)
    lbs = (_lower_bound(lbf_ref), _lower_bound(lbb_ref))
    n_pairs = HGRN_WIDTH // LANES
    n_chunks = IN_WIDTH // IN_CHUNK
    i_chunk = (QKV_WIDTH + 3 * HGRN_WIDTH) // IN_CHUNK
    order = [c for c in range(n_chunks) if c * IN_CHUNK >= QKV_WIDTH and c != i_chunk]
    order += [c for c in range(n_chunks) if c * IN_CHUNK < QKV_WIDTH] + [i_chunk]
    for c in order:
        acc = jnp.dot(xn, w_ref[:, c * IN_CHUNK:(c + 1) * IN_CHUNK], preferred_element_type=F32)
        for g in range(IN_CHUNK // LANES):
            lo = c * IN_CHUNK + g * LANES
            slab = lo // LANES
            group, pair = divmod((lo - QKV_WIDTH) // LANES, n_pairs)
            for rb in range(tm // IN_ROWS):
                rs = slice(rb * IN_ROWS, (rb + 1) * IN_ROWS)
                a = acc[rs, g * LANES:(g + 1) * LANES]
                if lo < QKV_WIDTH:
                    if lo < 2 * ATTN_WIDTH:
                        a = (a * rot_ref[0, rs, :] + pltpu.roll(a, LANES - ROT_HALF, 1) * rot_ref[1, rs, :]
                             + pltpu.roll(a, ROT_HALF, 1) * rot_ref[2, rs, :])
                        if lo < ATTN_WIDTH:
                            a = a * Q_SCALE
                    qkv_ref[slab, rs, :] = a.astype(BF16)
                    tile_a[rs, :] = a
                elif group == 0:
                    hf_ref[pair, rs, :] = a * _sigmoid(a)
                elif group == 4:
                    hb_ref[3 * n_pairs + pair, rs, :] = (a * _sigmoid(a)).astype(BF16)
                elif group == 3:
                    hb_ref[2 * n_pairs + pair, rs, :] = a.astype(BF16)
                else:
                    lb = lbs[group - 1][:, pair * LANES:(pair + 1) * LANES]
                    sig = _sigmoid(a)
                    hf_ref[group * n_pairs + pair, rs, :] = jnp.log(lb + (1.0 - lb) * sig)
                    hb_ref[(group - 1) * n_pairs + pair, rs, :] = (
                        (1.0 - lb) * (1.0 - sig)).astype(BF16)
            if lo < QKV_WIDTH:
                tiles = (tile_a, tile_b)
                for li in range(1, len(dils)):
                    d_prev, dil = dils[li - 1], dils[li]
                    ratio, len_prev, sub = dil // d_prev, tm // d_prev, tm // dil
                    t_in, t_out = tiles[(li - 1) % 2], tiles[li % 2]
                    for seg in range(d_prev):
                        for k in range(ratio):
                            rows = t_in[pl.ds(seg * len_prev + k, sub, stride=ratio), :]
                            res = seg + d_prev * k
                            perm_refs[li - 1][slab, res] = rows.astype(BF16)
                            if li + 1 < len(dils):
                                t_out[res * sub:(res + 1) * sub, :] = rows


def _rotary_tables(seq):
    inv_freq = ROPE_THETA ** (-jnp.arange(ROT_HALF, dtype=F32) * 2.0 / ROT_DIM)
    ang = jnp.arange(seq).astype(F32)[:, None] * inv_freq[None, :]
    cos, sin = jnp.cos(ang), jnp.sin(ang)
    zeros = jnp.zeros((seq, HEAD_DIM - ROT_DIM), F32)
    half0 = jnp.zeros((seq, ROT_HALF), F32)
    cos_h = jnp.concatenate([cos, cos, 1.0 + zeros], axis=1)
    lo_h = jnp.concatenate([-sin, half0, zeros], axis=1)
    hi_h = jnp.concatenate([half0, sin, zeros], axis=1)
    per_head = jnp.stack([cos_h, lo_h, hi_h], axis=0)
    return jnp.concatenate([per_head] * (LANES // HEAD_DIM), axis=2)


def _inproj(x2d, norm_w, w_in_bf16, rot, lb_fwd, lb_bwd, batch, seq, tm):
    tiles_per_seq = seq // tm
    n_qkv = QKV_WIDTH // LANES
    n_pairs = HGRN_WIDTH // LANES
    nslots = lb_fwd.shape[0]
    dils = tuple(d for _, d in DILATED_BRANCHES)
    assert dils[0] == 1 and all(b % a == 0 for a, b in zip(dils, dils[1:]))

    def slabs(n):
        return pl.BlockSpec((None, n, tm, LANES),
                            lambda i: (i // tiles_per_seq, 0, i % tiles_per_seq, 0))

    def residues(d):
        return pl.BlockSpec((None, n_qkv, d, tm // d, LANES),
                            lambda i: (i // tiles_per_seq, 0, 0, i % tiles_per_seq, 0))

    outs = pl.pallas_call(
        functools.partial(_inproj_body, dils=dils),
        grid=(batch * tiles_per_seq,),
        in_specs=[
            pl.BlockSpec((tm, D_MODEL), lambda i: (i, 0)),
            pl.BlockSpec((1, D_MODEL), lambda i: (0, 0)),
            pl.BlockSpec((D_MODEL, IN_WIDTH), lambda i: (0, 0), pipeline_mode=pl.Buffered(1)),
            pl.BlockSpec((3, tm, LANES), lambda i: (0, i % tiles_per_seq, 0)),
            pl.BlockSpec((nslots, HGRN_WIDTH), lambda i: (0, 0)),
            pl.BlockSpec((nslots, HGRN_WIDTH), lambda i: (0, 0)),
        ],
        out_specs=([slabs(n_qkv)] + [residues(d) for d in dils[1:]]
                   + [slabs(3 * n_pairs), slabs(4 * n_pairs)]),
        out_shape=([jax.ShapeDtypeStruct((batch, n_qkv, seq, LANES), BF16)]
                   + [jax.ShapeDtypeStruct((batch, n_qkv, d, seq // d, LANES), BF16)
                      for d in dils[1:]]
                   + [jax.ShapeDtypeStruct((batch, 3 * n_pairs, seq, LANES), F32),
                      jax.ShapeDtypeStruct((batch, 4 * n_pairs, seq, LANES), BF16)]),
        scratch_shapes=[pltpu.VMEM((tm, LANES), F32), pltpu.VMEM((tm, LANES), F32)],
        compiler_params=_cparams(("arbitrary",)),
        name="inproj",
    )(x2d, norm_w, w_in_bf16, rot, lb_fwd, lb_bwd)
    qkv_orders = [outs[0]] + [o.reshape(batch, n_qkv, seq, LANES) for o in outs[1:len(dils)]]
    return qkv_orders, outs[len(dils)], outs[len(dils) + 1]


ATTN_TQ = 128
ATTN_GROUP = 8
MERGE_ROWS = 1024


def _attn_body(*refs, seq):
    nb = len(DILATED_BRANCHES)
    qkv_refs = [refs[3 * bi:3 * bi + 3] for bi in range(nb)]
    out_ref, bias_ref, o_ref, lse_ref = refs[3 * nb:]
    lane = lax.broadcasted_iota(jnp.int32, (1, LANES), 1)
    head0 = lane < HEAD_DIM

    for bi, (window, dil) in enumerate(DILATED_BRANCHES):
        sub_len = seq // dil
        n_side = window // (2 * dil)
        tq = min(ATTN_TQ, sub_len)
        ks_len = min(sub_len, tq + 2 * n_side)
        nqb = sub_len // tq
        srcs = qkv_refs[bi]

        assert tq == 2 * n_side or nqb == 1
        heads = LANES // HEAD_DIM
        @pl.when(jnp.logical_and(pl.program_id(0) == 0, pl.program_id(1) == 0))
        def _(bi=bi, n_side=n_side, tq=tq, rows_b=heads * tq, ks_len=ks_len):
            row = lax.broadcasted_iota(jnp.int32, (rows_b, ks_len), 0) % tq
            col = lax.broadcasted_iota(jnp.int32, (rows_b, ks_len), 1)
            for di in range(3):
                inside = jnp.abs(row - col + di * n_side) <= n_side
                bias_ref[bi, di, 0:rows_b, 0:ks_len] = jnp.where(inside, 0.0, NEG_INF)

        def q_group(it, carry, bi=bi, dil=dil, sub_len=sub_len, n_side=n_side, tq=tq,
                    ks_len=ks_len, nqb=nqb, srcs=srcs, heads=heads):
            work = []
            for n in range(ATTN_GROUP):
                f = it * ATTN_GROUP + n
                r = f // nqb
                q0 = (f % nqb) * tq
                base = r * sub_len
                ks = jnp.clip(q0 - n_side, 0, sub_len - ks_len)
                q = srcs[0][pl.ds(pl.multiple_of(base + q0, tq), tq), :]
                k = srcs[1][pl.ds(pl.multiple_of(base + ks, n_side), ks_len), :]
                v = srcs[2][pl.ds(pl.multiple_of(base + ks, n_side), ks_len), :]
                zero = jnp.zeros_like(q)
                qq = jnp.concatenate([jnp.where(head0, q, zero), jnp.where(head0, zero, q)], axis=0)
                s = lax.dot_general(qq, k, (((1,), (1,)), ((), ())), preferred_element_type=F32)
                s = s + bias_ref[bi, (q0 - ks) // n_side, 0:heads * tq, 0:ks_len]
                if dil == 1:
                    rows = pl.ds(pl.multiple_of(q0, tq), tq)
                else:
                    rows = pl.ds(r + q0 * dil, tq, stride=dil)
                work.append((s, v, rows))
            for s, v, rows in work:
                m = jnp.max(s, axis=-1, keepdims=True)
                p = jnp.exp2(s - m)
                l = jnp.sum(p, axis=-1, keepdims=True)
                pv = jnp.dot(p.astype(BF16), v, preferred_element_type=F32)
                l = jnp.where(head0, l[0:tq], l[tq:])
                o_ref.at[bi][rows, :] = jnp.where(head0, pv[0:tq], pv[tq:]) * (1.0 / l)
                lse_ref.at[bi][rows, :] = jnp.where(head0, m[0:tq], m[tq:]) + jnp.log2(l)
            return carry

        lax.fori_loop(0, dil * nqb // ATTN_GROUP, q_group, 0)

    rows_m = min(MERGE_ROWS, seq)

    def merge(i, carry):
        rows = pl.ds(pl.multiple_of(i * rows_m, rows_m), rows_m)
        l1, l2, l3 = lse_ref.at[0][rows, :], lse_ref.at[1][rows, :], lse_ref.at[2][rows, :]
        m = jnp.maximum(jnp.maximum(l1, l2), l3)
        e1, e2, e3 = jnp.exp2(l1 - m), jnp.exp2(l2 - m), jnp.exp2(l3 - m)
        num = e1 * o_ref.at[0][rows, :] + e2 * o_ref.at[1][rows, :] + e3 * o_ref.at[2][rows, :]
        out_ref[rows, :] = (num * (1.0 / (e1 + e2 + e3))).astype(BF16)
        return carry

    lax.fori_loop(0, seq // rows_m, merge, 0)


def _attention(qkv_orders, batch, seq):
    nb = len(DILATED_BRANCHES)
    n_side = max(w // (2 * d) for w, d in DILATED_BRANCHES)

    def slab(k):
        return pl.BlockSpec((None, None, seq, LANES),
                            lambda b, p, k=k: (b, k * HEAD_PAIRS + p, 0, 0))

    operands = [arr for arr in qkv_orders for _ in range(3)]
    return pl.pallas_call(
        functools.partial(_attn_body, seq=seq),
        grid=(batch, HEAD_PAIRS),
        in_specs=[slab(k) for _ in qkv_orders for k in range(3)],
        out_specs=slab(0),
        out_shape=jax.ShapeDtypeStruct((batch, HEAD_PAIRS, seq, LANES), BF16),
        scratch_shapes=[pltpu.VMEM((nb, 3, LANES // HEAD_DIM * ATTN_TQ, ATTN_TQ + 2 * n_side), F32),
                        pltpu.VMEM((nb, seq, LANES), F32),
                        pltpu.VMEM((nb, seq, LANES), F32)],
        compiler_params=_cparams(("arbitrary", "arbitrary")),
        name="attention",
    )(*operands)


HGRN_UNROLL = 32
HGRN_PREP_ROWS = 1024


def _chunk_cumsum(g, reverse):
    rows = g.shape[0]
    pos = lax.broadcasted_iota(jnp.int32, (rows, 1), 0) % HGRN_CHUNK
    sh = 1
    while sh < HGRN_CHUNK:
        if reverse:
            shifted = pltpu.roll(g, rows - sh, 0)
            valid = pos < HGRN_CHUNK - sh
        else:
            shifted = pltpu.roll(g, sh, 0)
            valid = pos >= sh
        g = g + jnp.where(valid, shifted, 0.0)
        sh *= 2
    return g


def _split_heads(x, head0):
    zero = jnp.zeros_like(x)
    return jnp.concatenate([jnp.where(head0, x, zero), jnp.where(head0, zero, x)], axis=0)


def _causal_keep(reverse):
    t_l = lax.broadcasted_iota(jnp.int32, (HGRN_CHUNK, LANES), 0)
    s_l = lax.broadcasted_iota(jnp.int32, (HGRN_CHUNK, LANES), 1) % HEAD_DIM
    return (s_l >= t_l) if reverse else (s_l <= t_l)


def _hgrn_fast(q_ref, g_ref, kk_ref, vb_ref, vt_ref, oacc_ref, qt_ref, kt_ref, ke_ref, dec_ref,
               *, seq, reverse):
    cs = HGRN_CHUNK
    n_chunks = seq // cs
    prep = HGRN_PREP_ROWS if seq % HGRN_PREP_ROWS == 0 else cs * SUBLANES
    cpb = prep // cs
    head0 = lax.broadcasted_iota(jnp.int32, (1, LANES), 1) < HEAD_DIM
    same_head = _head_block_ones()
    t_h = lax.broadcasted_iota(jnp.int32, (2 * cs, cs), 0) % cs
    s_h = lax.broadcasted_iota(jnp.int32, (2 * cs, cs), 1)
    keep = (s_h >= t_h) if reverse else (s_h <= t_h)

    t_i = lax.broadcasted_iota(jnp.int32, (cs, cs), 0)
    s_i = lax.broadcasted_iota(jnp.int32, (cs, cs), 1)
    tri = ((s_i >= t_i) if reverse else (s_i <= t_i)).astype(BF16)

    def cumsum(g):
        hi = g.astype(BF16)
        lo = (g - hi.astype(F32)).astype(BF16)
        split = jnp.concatenate([hi, lo], axis=1)
        parts = []
        for c in range(cpb):
            both = jnp.dot(tri, split[c * cs:(c + 1) * cs], preferred_element_type=F32)
            parts.append(both[:, :LANES] + both[:, LANES:])
        return jnp.concatenate(parts, axis=0)

    def prepare(i, carry):
        rows = pl.ds(pl.multiple_of(i * prep, prep), prep)
        g = g_ref[rows, :]
        kk = kk_ref[rows, :].astype(F32)
        a = cumsum(g)
        a_end = jnp.sum(g.reshape(cpb, cs, LANES), axis=1)
        ea = jnp.exp(a)
        qt_ref[rows, :] = (q_ref[rows, :] * ea).astype(BF16)
        kt_ref[rows, :] = (kk * (1.0 / ea)).astype(BF16)
        to_end = jnp.exp(a_end[:, None, :] - a.reshape(cpb, cs, LANES)).reshape(prep, LANES)
        ke_ref[rows, :] = (kk * to_end).astype(BF16)
        dec_ref[pl.ds(pl.multiple_of(i * cpb, cpb), cpb), :] = jnp.exp(a_end)
        return carry

    lax.fori_loop(0, seq // prep, prepare, 0)

    unroll = math.gcd(HGRN_UNROLL, n_chunks)

    def step(i, st):
        chunks = [i * unroll + u for u in range(unroll)]
        if reverse:
            chunks = [n_chunks - 1 - j for j in chunks]
        rows = [pl.ds(pl.multiple_of(c * cs, cs), cs) for c in chunks]
        upds = [jnp.dot(vt_ref[pl.ds(pl.multiple_of(c * LANES, LANES), LANES), :], ke_ref[r, :],
                        preferred_element_type=F32) for c, r in zip(chunks, rows)]
        atts = [lax.dot_general(_split_heads(qt_ref[r, :], head0), kt_ref[r, :],
                                (((1,), (1,)), ((), ())), preferred_element_type=F32)
                for r in rows]
        states = []
        for c, upd in zip(chunks, upds):
            states.append(st.astype(BF16))
            st = jnp.where(same_head, st * dec_ref[pl.ds(c, 1), :] + upd, 0.0)
        for r, att, st_in in zip(rows, atts, states):
            att = jnp.where(keep, att, 0.0).astype(BF16)
            both = jnp.dot(att, vb_ref[r, :], preferred_element_type=F32)
            o = (jnp.where(head0, both[0:cs], both[cs:])
                 + lax.dot_general(qt_ref[r, :], st_in, (((1,), (1,)), ((), ())),
                                   preferred_element_type=F32))
            if reverse:
                oacc_ref[r, :] += o
            else:
                oacc_ref[r, :] = o
        return st

    lax.fori_loop(0, n_chunks // unroll, step, jnp.zeros((LANES, LANES), F32))


def _hgrn_safe(q_ref, g_ref, kk_ref, vb_ref, oacc_ref, *, seq, reverse):
    cs = HGRN_CHUNK
    n_chunks = seq // cs
    head0 = lax.broadcasted_iota(jnp.int32, (1, LANES), 1) < HEAD_DIM
    same_head = _head_block_ones()
    keep = _causal_keep(reverse)
    col = lax.broadcasted_iota(jnp.int32, (LANES, LANES), 1)
    last = 0 if reverse else cs - 1

    def chunk(j, st):
        c = n_chunks - 1 - j if reverse else j
        rows = pl.ds(pl.multiple_of(c * cs, cs), cs)
        g = g_ref[rows, :]
        kk = kk_ref[rows, :].astype(F32)
        q = q_ref[rows, :]
        vb = vb_ref[rows, :]
        a = _chunk_cumsum(g, reverse)
        a_end = a[last:last + 1, :]
        qt = (q * jnp.exp(a)).astype(BF16)

        def column(s, att):
            sel = lax.broadcasted_iota(jnp.int32, (cs, 1), 0) == s
            a_s = jnp.sum(jnp.where(sel, a, 0.0), axis=0, keepdims=True)
            kk_s = jnp.sum(jnp.where(sel, kk, 0.0), axis=0, keepdims=True)
            e = jnp.exp(jnp.minimum(a - a_s, 0.0)) * (q * kk_s)
            place = jnp.logical_and(same_head, col % HEAD_DIM == s).astype(F32)
            return att + jnp.dot(e, place, preferred_element_type=F32,
                                 precision=lax.Precision.HIGHEST)

        att = lax.fori_loop(0, cs, column, jnp.zeros((cs, LANES), F32))
        att = jnp.where(keep, att, 0.0).astype(BF16)
        o = (jnp.dot(att, _split_heads(vb, head0), preferred_element_type=F32)
             + lax.dot_general(qt, st.astype(BF16), (((1,), (1,)), ((), ())),
                               preferred_element_type=F32))
        k_end = (kk * jnp.exp(a_end - a)).astype(BF16)
        upd = lax.dot_general(vb, k_end, (((0,), (0,)), ((), ())), preferred_element_type=F32)
        st = jnp.where(same_head, st * jnp.exp(a_end) + upd, 0.0)
        if reverse:
            oacc_ref[rows, :] += o
        else:
            oacc_ref[rows, :] = o
        return st

    lax.fori_loop(0, n_chunks, chunk, jnp.zeros((LANES, LANES), F32))


def _hgrn_body(q_ref, gf_ref, gb_ref, kf_ref, kb_ref, vb_ref, gate_ref, lbf_ref, lbb_ref, nw_ref,
               out_ref, oacc_ref, vt_ref, qt_ref, kt_ref, ke_ref, dec_ref, *, seq):
    rows_n = HGRN_PREP_ROWS if seq % HGRN_PREP_ROWS == 0 else LANES

    def transpose_v(i, carry):
        for t in range(rows_n // HGRN_CHUNK):
            src = pl.ds(pl.multiple_of(i * rows_n + t * HGRN_CHUNK, HGRN_CHUNK), HGRN_CHUNK)
            dst = pl.ds(pl.multiple_of(2 * i * rows_n + t * LANES, LANES), LANES)
            vt_ref[dst, :] = vb_ref[src, :].astype(F32).T.astype(BF16)
        return carry

    lax.fori_loop(0, seq // rows_n, transpose_v, 0)
    for g_ref, kk_ref, lbp_ref, reverse in ((gf_ref, kf_ref, lbf_ref, False),
                                            (gb_ref, kb_ref, lbb_ref, True)):
        worst = jnp.max(-jnp.log(_lower_bound(lbp_ref))) * HGRN_CHUNK

        @pl.when(worst <= F32_EXP_SAFE)
        def _():
            _hgrn_fast(q_ref, g_ref, kk_ref, vb_ref, vt_ref, oacc_ref, qt_ref, kt_ref, ke_ref,
                       dec_ref, seq=seq, reverse=reverse)

        @pl.when(jnp.logical_not(worst <= F32_EXP_SAFE))
        def _():
            _hgrn_safe(q_ref, g_ref, kk_ref, vb_ref, oacc_ref, seq=seq, reverse=reverse)

    ones_blk = _head_block_ones().astype(BF16)

    def finish(i, carry):
        rows = pl.ds(pl.multiple_of(i * rows_n, rows_n), rows_n)
        o = oacc_ref[rows, :]
        sq = o * o
        hi = sq.astype(BF16)
        lo = (sq - hi.astype(F32)).astype(BF16)
        ms = (jnp.dot(hi, ones_blk, preferred_element_type=F32)
              + jnp.dot(lo, ones_blk, preferred_element_type=F32)) * (1.0 / HEAD_DIM)
        y = o * lax.rsqrt(ms + NORM_EPS) * nw_ref[...] * gate_ref[rows, :].astype(F32)
        out_ref[rows, :] = y.astype(BF16)
        return carry

    lax.fori_loop(0, seq // rows_n, finish, 0)


def _hgrn(hf, hb, lb_fwd, lb_bwd, out_norm_w, batch, seq):
    nslots = lb_fwd.shape[0]
    cols = HGRN_WIDTH // LANES

    def slab(k):
        return pl.BlockSpec((None, None, seq, LANES), lambda b, p, k=k: (b, k * cols + p, 0, 0))

    return pl.pallas_call(
        functools.partial(_hgrn_body, seq=seq),
        grid=(batch, cols),
        in_specs=[slab(0), slab(1), slab(2), slab(0), slab(1), slab(2), slab(3),
                  pl.BlockSpec((nslots, LANES), lambda b, p: (0, p)),
                  pl.BlockSpec((nslots, LANES), lambda b, p: (0, p)),
                  pl.BlockSpec((1, LANES), lambda b, p: (0, 0))],
        out_specs=slab(0),
        out_shape=jax.ShapeDtypeStruct((batch, cols, seq, LANES), BF16),
        scratch_shapes=[pltpu.VMEM((seq, LANES), F32),
                        pltpu.VMEM((seq // HGRN_CHUNK * LANES, HGRN_CHUNK), BF16),
                        pltpu.VMEM((seq, LANES), BF16),
                        pltpu.VMEM((seq, LANES), BF16),
                        pltpu.VMEM((seq, LANES), BF16),
                        pltpu.VMEM((seq // HGRN_CHUNK, LANES), F32)],
        compiler_params=_cparams(("arbitrary", "arbitrary")),
        name="hgrn",
    )(hf, hf, hf, hb, hb, hb, hb, lb_fwd, lb_bwd, out_norm_w)


FF_CHUNK = 256
HALO = 16


def _mix_ffn_body(ap_ref, a_ref, an_ref, gp_ref, g_ref, gn_ref, xp_ref, x_ref, xn_ref,
                  wo_ref, nw_mix_ref, nw_pre_ref, wg_ref, wu_ref, cw_ref, cb_ref, wd_ref,
                  nw_post_ref, y_ref, act_ref, *, tm, tiles_per_seq):
    i = pl.program_id(0)
    ext = tm + 2 * HALO

    def mixer_rows(attn_ref, hgrn_ref):
        return jnp.concatenate([attn_ref[j] for j in range(attn_ref.shape[0])]
                               + [hgrn_ref[j] for j in range(hgrn_ref.shape[0])], axis=1)

    mixed = jnp.concatenate([mixer_rows(ap_ref, gp_ref), mixer_rows(a_ref, g_ref),
                             mixer_rows(an_ref, gn_ref)], axis=0)
    mix = jnp.dot(mixed, wo_ref[...], preferred_element_type=F32)
    x_ext = jnp.concatenate([xp_ref[...], x_ref[...], xn_ref[...]], axis=0)
    h_ext = x_ext + mix * _rms_scale(mix) * nw_mix_ref[...]
    h = h_ext[HALO:HALO + tm]

    row = lax.broadcasted_iota(jnp.int32, (ext, 1), 0)
    has_prev = (i % tiles_per_seq != 0).astype(F32)
    has_next = (i % tiles_per_seq != tiles_per_seq - 1).astype(F32)
    live = jnp.where(row < HALO, has_prev, jnp.where(row >= HALO + tm, has_next, 1.0))
    xn = h_ext * _rms_scale(h_ext) * nw_pre_ref[...] * live
    xn_ext = xn.astype(BF16)
    xn_main = xn[HALO:HALO + tm].astype(BF16)
    for c in range(D_FF // FF_CHUNK):
        cols = slice(c * FF_CHUNK, (c + 1) * FF_CHUNK)
        a = jnp.dot(xn_ext, wg_ref[:, cols], preferred_element_type=F32)
        a_prev = pltpu.roll(a, 1, 0)[HALO:HALO + tm]
        a_next = pltpu.roll(a, ext - 1, 0)[HALO:HALO + tm]
        a_mid = a[HALO:HALO + tm]
        conv = (a_prev * cw_ref[0:1, cols] + a_mid * cw_ref[1:2, cols]
                + a_next * cw_ref[2:3, cols] + cb_ref[:, cols])
        b = jnp.dot(xn_main, wu_ref[:, cols], preferred_element_type=F32)
        gelu = 0.5 * conv * (1.0 + jnp.tanh(math.sqrt(2.0 / math.pi)
                                            * (conv + 0.044715 * (conv * conv * conv))))
        act_ref[:, cols] = (gelu * b).astype(BF16)
    ffn = jnp.dot(act_ref[...], wd_ref[...], preferred_element_type=F32)
    y_ref[...] = h + ffn * _rms_scale(ffn) * nw_post_ref[...]


def _mix_ffn(attn, hg, x2d, p, seq, tm):
    tokens = x2d.shape[0]
    tiles_per_seq = seq // tm
    hb = tm // HALO
    halos_per_seq = seq // HALO
    n_halo = tokens // HALO
    resident = dict(pipeline_mode=pl.Buffered(1))

    def slabs(rows, row_block):
        return pl.BlockSpec((None, HEAD_PAIRS, rows, LANES),
                            lambda i: (i // tiles_per_seq, 0, row_block(i % tiles_per_seq), 0))

    def prev_block(t):
        return jnp.maximum(t * hb - 1, 0)

    def next_block(t):
        return jnp.minimum((t + 1) * hb, halos_per_seq - 1)

    head_group = [slabs(HALO, prev_block), slabs(tm, lambda t: t), slabs(HALO, next_block)]
    row_vec = lambda n: pl.BlockSpec((1, n), lambda i: (0, 0))
    return pl.pallas_call(
        functools.partial(_mix_ffn_body, tm=tm, tiles_per_seq=tiles_per_seq),
        grid=(tokens // tm,),
        in_specs=head_group + head_group + [
            pl.BlockSpec((HALO, D_MODEL), lambda i: (jnp.maximum(i * hb - 1, 0), 0)),
            pl.BlockSpec((tm, D_MODEL), lambda i: (i, 0)),
            pl.BlockSpec((HALO, D_MODEL), lambda i: (jnp.minimum((i + 1) * hb, n_halo - 1), 0)),
            pl.BlockSpec((D_MODEL, D_MODEL), lambda i: (0, 0), **resident),
            row_vec(D_MODEL),
            row_vec(D_MODEL),
            pl.BlockSpec((D_MODEL, D_FF), lambda i: (0, 0), **resident),
            pl.BlockSpec((D_MODEL, D_FF), lambda i: (0, 0), **resident),
            pl.BlockSpec((3, D_FF), lambda i: (0, 0)),
            row_vec(D_FF),
            pl.BlockSpec((D_FF, D_MODEL), lambda i: (0, 0), **resident),
            row_vec(D_MODEL),
        ],
        out_specs=pl.BlockSpec((tm, D_MODEL), lambda i: (i, 0)),
        out_shape=jax.ShapeDtypeStruct((tokens, D_MODEL), F32),
        scratch_shapes=[pltpu.VMEM((tm, D_FF), BF16)],
        compiler_params=_cparams(("arbitrary",)),
        name="mix_ffn",
    )(attn, attn, attn, hg, hg, hg, x2d, x2d, x2d, p["w_out"], p["norm_mix_post"],
      p["norm_ffn_pre"], p["w_gate"], p["w_up"], p["conv_w"], p["conv_b"], p["w_down"],
      p["norm_ffn_post"])


def _token_tile(seq):
    return 512 if seq % 512 == 0 else seq


def _encode(x, p):
    batch, seq, _ = x.shape
    tm = _token_tile(seq)
    x2d = x.reshape(batch * seq, D_MODEL)
    qkv_orders, hf, hb = _inproj(x2d, p["norm_mix_pre"], p["w_in"], _rotary_tables(seq),
                                 p["lb_fwd"], p["lb_bwd"], batch, seq, tm)
    attn = _attention(qkv_orders, batch, seq)
    hg = _hgrn(hf, hb, p["lb_fwd"], p["lb_bwd"], p["hgrn_out_norm"], batch, seq)
    y = _mix_ffn(attn, hg, x2d, p, seq, tm)
    return y.reshape(batch, seq, D_MODEL)


def kernel(x_prompt, x_sample, norm_mix_pre, w_in, hgrn_lb_fwd, hgrn_lb_bwd, hgrn_out_norm, w_out,
           norm_mix_post, norm_ffn_pre, w_gate, w_up, conv_w, conv_b, w_down, norm_ffn_post):
    assert w_in.shape[0] == 1, "one layer"
    p = {
        "norm_mix_pre": norm_mix_pre[0][None, :],
        "w_in": w_in[0].astype(BF16),
        "lb_fwd": hgrn_lb_fwd.astype(F32),
        "lb_bwd": hgrn_lb_bwd.astype(F32),
        "hgrn_out_norm": jnp.tile(hgrn_out_norm[0], LANES // HEAD_DIM)[None, :],
        "w_out": w_out[0].astype(BF16),
        "norm_mix_post": norm_mix_post[0][None, :],
        "norm_ffn_pre": norm_ffn_pre[0][None, :],
        "w_gate": w_gate[0].astype(BF16),
        "w_up": w_up[0].astype(BF16),
        "conv_w": conv_w[0],
        "conv_b": conv_b[0][None, :],
        "w_down": w_down[0].astype(BF16),
        "norm_ffn_post": norm_ffn_post[0][None, :],
    }
    return (_encode(x_prompt, p), _encode(x_sample, p))
```

```python
import functools
import math

import jax
import jax.numpy as jnp
from jax import lax
from jax.experimental import pallas as pl
from jax.experimental.pallas import tpu as pltpu

F32 = jnp.float32
BF16 = jnp.bfloat16

D_MODEL = 1024
HEAD_DIM = 64
ATTN_HEADS = 8
HGRN_HEADS = 8
ATTN_WIDTH = ATTN_HEADS * HEAD_DIM
HGRN_WIDTH = HGRN_HEADS * HEAD_DIM
QKV_WIDTH = 3 * ATTN_WIDTH
HPROJ_WIDTH = 5 * HGRN_WIDTH
IN_WIDTH = QKV_WIDTH + HPROJ_WIDTH
DILATED_BRANCHES = ((128, 1), (512, 4), (2048, 16))
ROPE_THETA = 500000.0
ROT_DIM = HEAD_DIM // 4
ROT_HALF = ROT_DIM // 2
HGRN_CHUNK = 64
D_FF = 2816
NORM_EPS = 1e-6
NEG_INF = -1e30

LANES = 128
SUBLANES = 8
HEAD_PAIRS = ATTN_WIDTH // LANES
VMEM_LIMIT_BYTES = 56 * 1024 * 1024
F32_EXP_SAFE = 80.0


def _cparams(sem):
    return pltpu.CompilerParams(dimension_semantics=sem, vmem_limit_bytes=VMEM_LIMIT_BYTES)


def _rms_scale(x):
    return lax.rsqrt(jnp.mean(x * x, axis=-1, keepdims=True) + NORM_EPS)


def _sigmoid(z):
    return 1.0 / (1.0 + jnp.exp(-z))


def _head_block_ones():
    r = lax.broadcasted_iota(jnp.int32, (LANES, LANES), 0) // HEAD_DIM
    c = lax.broadcasted_iota(jnp.int32, (LANES, LANES), 1) // HEAD_DIM
    return r == c


IN_CHUNK = 256
IN_ROWS = 128
Q_SCALE = math.log2(math.e) / math.sqrt(HEAD_DIM)


def _lower_bound(lbp_ref):
    p = lbp_ref[...]
    e = jnp.exp(p - jnp.max(p, axis=0, keepdims=True))
    return e[0:1, :] / jnp.sum(e, axis=0, keepdims=True)


def _inproj_body(x_ref, nw_ref, w_ref, rot_ref, lbf_ref, lbb_ref, *refs, dils):
    qkv_ref = refs[0]
    perm_refs = refs[1:len(dils)]
    hf_ref, hb_ref, tile_a, tile_b = refs[len(dils):]
    tm = x_ref.shape[0]
    x = x_ref[...]
    xn = (x * _rms_scale(x) * nw_ref[...]).astype(BF16)
    lbs = (_lower_bound(lbf_ref), _lower_bound(lbb_ref))
    n_pairs = HGRN_WIDTH // LANES
    n_chunks = IN_WIDTH // IN_CHUNK
    i_chunk = (QKV_WIDTH + 3 * HGRN_WIDTH) // IN_CHUNK
    order = [c for c in range(n_chunks) if c * IN_CHUNK >= QKV_WIDTH and c != i_chunk]
    order += [c for c in range(n_chunks) if c * IN_CHUNK < QKV_WIDTH] + [i_chunk]
    for c in order:
        acc = jnp.dot(xn, w_ref[:, c * IN_CHUNK:(c + 1) * IN_CHUNK], preferred_element_type=F32)
        for g in range(IN_CHUNK // LANES):
            lo = c * IN_CHUNK + g * LANES
            slab = lo // LANES
            group, pair = divmod((lo - QKV_WIDTH) // LANES, n_pairs)
            for rb in range(tm // IN_ROWS):
                rs = slice(rb * IN_ROWS, (rb + 1) * IN_ROWS)
                a = acc[rs, g * LANES:(g + 1) * LANES]
                if lo < QKV_WIDTH:
                    if lo < 2 * ATTN_WIDTH:
                        a = (a * rot_ref[0, rs, :] + pltpu.roll(a, LANES - ROT_HALF, 1) * rot_ref[1, rs, :]
                             + pltpu.roll(a, ROT_HALF, 1) * rot_ref[2, rs, :])
                        if lo < ATTN_WIDTH:
                            a = a * Q_SCALE
                    qkv_ref[slab, rs, :] = a.astype(BF16)
                    tile_a[rs, :] = a
                elif group == 0:
                    hf_ref[pair, rs, :] = a * _sigmoid(a)
                elif group == 4:
                    hb_ref[3 * n_pairs + pair, rs, :] = (a * _sigmoid(a)).astype(BF16)
                elif group == 3:
                    hb_ref[2 * n_pairs + pair, rs, :] = a.astype(BF16)
                else:
                    lb = lbs[group - 1][:, pair * LANES:(pair + 1) * LANES]
                    sig = _sigmoid(a)
                    hf_ref[group * n_pairs + pair, rs, :] = jnp.log(lb + (1.0 - lb) * sig)
                    hb_ref[(group - 1) * n_pairs + pair, rs, :] = (
                        (1.0 - lb) * (1.0 - sig)).astype(BF16)
            if lo < QKV_WIDTH:
                tiles = (tile_a, tile_b)
                for li in range(1, len(dils)):
                    d_prev, dil = dils[li - 1], dils[li]
                    ratio, len_prev, sub = dil // d_prev, tm // d_prev, tm // dil
                    t_in, t_out = tiles[(li - 1) % 2], tiles[li % 2]
                    for seg in range(d_prev):
                        for k in range(ratio):
                            rows = t_in[pl.ds(seg * len_prev + k, sub, stride=ratio), :]
                            res = seg + d_prev * k
                            perm_refs[li - 1][slab, res] = rows.astype(BF16)
                            if li + 1 < len(dils):
                                t_out[res * sub:(res + 1) * sub, :] = rows


def _rotary_tables(seq):
    inv_freq = ROPE_THETA ** (-jnp.arange(ROT_HALF, dtype=F32) * 2.0 / ROT_DIM)
    ang = jnp.arange(seq).astype(F32)[:, None] * inv_freq[None, :]
    cos, sin = jnp.cos(ang), jnp.sin(ang)
    zeros = jnp.zeros((seq, HEAD_DIM - ROT_DIM), F32)
    half0 = jnp.zeros((seq, ROT_HALF), F32)
    cos_h = jnp.concatenate([cos, cos, 1.0 + zeros], axis=1)
    lo_h = jnp.concatenate([-sin, half0, zeros], axis=1)
    hi_h = jnp.concatenate([half0, sin, zeros], axis=1)
    per_head = jnp.stack([cos_h, lo_h, hi_h], axis=0)
    return jnp.concatenate([per_head] * (LANES // HEAD_DIM), axis=2)


def _inproj(x2d, norm_w, w_in_bf16, rot, lb_fwd, lb_bwd, batch, seq, tm):
    tiles_per_seq = seq // tm
    n_qkv = QKV_WIDTH // LANES
    n_pairs = HGRN_WIDTH // LANES
    nslots = lb_fwd.shape[0]
    dils = tuple(d for _, d in DILATED_BRANCHES)
    assert dils[0] == 1 and all(b % a == 0 for a, b in zip(dils, dils[1:]))

    def slabs(n):
        return pl.BlockSpec((None, n, tm, LANES),
                            lambda i: (i // tiles_per_seq, 0, i % tiles_per_seq, 0))

    def residues(d):
        return pl.BlockSpec((None, n_qkv, d, tm // d, LANES),
                            lambda i: (i // tiles_per_seq, 0, 0, i % tiles_per_seq, 0))

    outs = pl.pallas_call(
        functools.partial(_inproj_body, dils=dils),
        grid=(batch * tiles_per_seq,),
        in_specs=[
            pl.BlockSpec((tm, D_MODEL), lambda i: (i, 0)),
            pl.BlockSpec((1, D_MODEL), lambda i: (0, 0)),
            pl.BlockSpec((D_MODEL, IN_WIDTH), lambda i: (0, 0), pipeline_mode=pl.Buffered(1)),
            pl.BlockSpec((3, tm, LANES), lambda i: (0, i % tiles_per_seq, 0)),
            pl.BlockSpec((nslots, HGRN_WIDTH), lambda i: (0, 0)),
            pl.BlockSpec((nslots, HGRN_WIDTH), lambda i: (0, 0)),
        ],
        out_specs=([slabs(n_qkv)] + [residues(d) for d in dils[1:]]
                   + [slabs(3 * n_pairs), slabs(4 * n_pairs)]),
        out_shape=([jax.ShapeDtypeStruct((batch, n_qkv, seq, LANES), BF16)]
                   + [jax.ShapeDtypeStruct((batch, n_qkv, d, seq // d, LANES), BF16)
                      for d in dils[1:]]
                   + [jax.ShapeDtypeStruct((batch, 3 * n_pairs, seq, LANES), F32),
                      jax.ShapeDtypeStruct((batch, 4 * n_pairs, seq, LANES), BF16)]),
        scratch_shapes=[pltpu.VMEM((tm, LANES), F32), pltpu.VMEM((tm, LANES), F32)],
        compiler_params=_cparams(("arbitrary",)),
        name="inproj",
    )(x2d, norm_w, w_in_bf16, rot, lb_fwd, lb_bwd)
    qkv_orders = [outs[0]] + [o.reshape(batch, n_qkv, seq, LANES) for o in outs[1:len(dils)]]
    return qkv_orders, outs[len(dils)], outs[len(dils) + 1]


ATTN_TQ = 128
ATTN_GROUP = 8
MERGE_ROWS = 1024


def _attn_body(*refs, seq):
    nb = len(DILATED_BRANCHES)
    qkv_refs = [refs[3 * bi:3 * bi + 3] for bi in range(nb)]
    out_ref, bias_ref, o_ref, lse_ref = refs[3 * nb:]
    lane = lax.broadcasted_iota(jnp.int32, (1, LANES), 1)
    head0 = lane < HEAD_DIM

    for bi, (window, dil) in enumerate(DILATED_BRANCHES):
        sub_len = seq // dil
        n_side = window // (2 * dil)
        tq = min(ATTN_TQ, sub_len)
        ks_len = min(sub_len, tq + 2 * n_side)
        nqb = sub_len // tq
        srcs = qkv_refs[bi]

        assert tq == 2 * n_side or nqb == 1
        heads = LANES // HEAD_DIM
        @pl.when(jnp.logical_and(pl.program_id(0) == 0, pl.program_id(1) == 0))
        def _(bi=bi, n_side=n_side, tq=tq, rows_b=heads * tq, ks_len=ks_len):
            row = lax.broadcasted_iota(jnp.int32, (rows_b, ks_len), 0) % tq
            col = lax.broadcasted_iota(jnp.int32, (rows_b, ks_len), 1)
            for di in range(3):
                inside = jnp.abs(row - col + di * n_side) <= n_side
                bias_ref[bi, di, 0:rows_b, 0:ks_len] = jnp.where(inside, 0.0, NEG_INF)

        def q_group(it, carry, bi=bi, dil=dil, sub_len=sub_len, n_side=n_side, tq=tq,
                    ks_len=ks_len, nqb=nqb, srcs=srcs, heads=heads):
            work = []
            for n in range(ATTN_GROUP):
                f = it * ATTN_GROUP + n
                r = f // nqb
                q0 = (f % nqb) * tq
                base = r * sub_len
                ks = jnp.clip(q0 - n_side, 0, sub_len - ks_len)
                q = srcs[0][pl.ds(pl.multiple_of(base + q0, tq), tq), :]
                k = srcs[1][pl.ds(pl.multiple_of(base + ks, n_side), ks_len), :]
                v = srcs[2][pl.ds(pl.multiple_of(base + ks, n_side), ks_len), :]
                zero = jnp.zeros_like(q)
                qq = jnp.concatenate([jnp.where(head0, q, zero), jnp.where(head0, zero, q)], axis=0)
                s = lax.dot_general(qq, k, (((1,), (1,)), ((), ())), preferred_element_type=F32)
                s = s + bias_ref[bi, (q0 - ks) // n_side, 0:heads * tq, 0:ks_len]
                if dil == 1:
                    rows = pl.ds(pl.multiple_of(q0, tq), tq)
                else:
                    rows = pl.ds(r + q0 * dil, tq, stride=dil)
                work.append((s, v, rows))
            for s, v, rows in work:
                m = jnp.max(s, axis=-1, keepdims=True)
                p = jnp.exp2(s - m)
                l = jnp.sum(p, axis=-1, keepdims=True)
                pv = jnp.dot(p.astype(BF16), v, preferred_element_type=F32)
                l = jnp.where(head0, l[0:tq], l[tq:])
                o_ref.at[bi][rows, :] = jnp.where(head0, pv[0:tq], pv[tq:]) * (1.0 / l)
                lse_ref.at[bi][rows, :] = jnp.where(head0, m[0:tq], m[tq:]) + jnp.log2(l)
            return carry

        lax.fori_loop(0, dil * nqb // ATTN_GROUP, q_group, 0)

    rows_m = min(MERGE_ROWS, seq)

    def merge(i, carry):
        rows = pl.ds(pl.multiple_of(i * rows_m, rows_m), rows_m)
        l1, l2, l3 = lse_ref.at[0][rows, :], lse_ref.at[1][rows, :], lse_ref.at[2][rows, :]
        m = jnp.maximum(jnp.maximum(l1, l2), l3)
        e1, e2, e3 = jnp.exp2(l1 - m), jnp.exp2(l2 - m), jnp.exp2(l3 - m)
        num = e1 * o_ref.at[0][rows, :] + e2 * o_ref.at[1][rows, :] + e3 * o_ref.at[2][rows, :]
        out_ref[rows, :] = (num * (1.0 / (e1 + e2 + e3))).astype(BF16)
        return carry

    lax.fori_loop(0, seq // rows_m, merge, 0)


def _attention(qkv_orders, batch, seq):
    nb = len(DILATED_BRANCHES)
    n_side = max(w // (2 * d) for w, d in DILATED_BRANCHES)

    def slab(k):
        return pl.BlockSpec((None, None, seq, LANES),
                            lambda b, p, k=k: (b, k * HEAD_PAIRS + p, 0, 0))

    operands = [arr for arr in qkv_orders for _ in range(3)]
    return pl.pallas_call(
        functools.partial(_attn_body, seq=seq),
        grid=(batch, HEAD_PAIRS),
        in_specs=[slab(k) for _ in qkv_orders for k in range(3)],
        out_specs=slab(0),
        out_shape=jax.ShapeDtypeStruct((batch, HEAD_PAIRS, seq, LANES), BF16),
        scratch_shapes=[pltpu.VMEM((nb, 3, LANES // HEAD_DIM * ATTN_TQ, ATTN_TQ + 2 * n_side), F32),
                        pltpu.VMEM((nb, seq, LANES), F32),
                        pltpu.VMEM((nb, seq, LANES), F32)],
        compiler_params=_cparams(("arbitrary", "arbitrary")),
        name="attention",
    )(*operands)


HGRN_UNROLL = 64
HGRN_PREP_ROWS = 1024


def _chunk_cumsum(g, reverse):
    rows = g.shape[0]
    pos = lax.broadcasted_iota(jnp.int32, (rows, 1), 0) % HGRN_CHUNK
    sh = 1
    while sh < HGRN_CHUNK:
        if reverse:
            shifted = pltpu.roll(g, rows - sh, 0)
            valid = pos < HGRN_CHUNK - sh
        else:
            shifted = pltpu.roll(g, sh, 0)
            valid = pos >= sh
        g = g + jnp.where(valid, shifted, 0.0)
        sh *= 2
    return g


def _split_heads(x, head0):
    zero = jnp.zeros_like(x)
    return jnp.concatenate([jnp.where(head0, x, zero), jnp.where(head0, zero, x)], axis=0)


def _causal_keep(reverse):
    t_l = lax.broadcasted_iota(jnp.int32, (HGRN_CHUNK, LANES), 0)
    s_l = lax.broadcasted_iota(jnp.int32, (HGRN_CHUNK, LANES), 1) % HEAD_DIM
    return (s_l >= t_l) if reverse else (s_l <= t_l)


def _hgrn_prepare(q_ref, g_ref, kk_ref, qt_ref, kt_ref, ke_ref, dec_ref, *, seq, reverse):
    cs = HGRN_CHUNK
    prep = HGRN_PREP_ROWS if seq % HGRN_PREP_ROWS == 0 else cs * SUBLANES
    cpb = prep // cs
    t_i = lax.broadcasted_iota(jnp.int32, (cs, cs), 0)
    s_i = lax.broadcasted_iota(jnp.int32, (cs, cs), 1)
    tri = ((s_i >= t_i) if reverse else (s_i <= t_i)).astype(BF16)

    def cumsum(g):
        hi = g.astype(BF16)
        lo = (g - hi.astype(F32)).astype(BF16)
        split = jnp.concatenate([hi, lo], axis=1)
        parts = []
        for c in range(cpb):
            both = jnp.dot(tri, split[c * cs:(c + 1) * cs], preferred_element_type=F32)
            parts.append(both[:, :LANES] + both[:, LANES:])
        return jnp.concatenate(parts, axis=0)

    def prepare(i, carry):
        rows = pl.ds(pl.multiple_of(i * prep, prep), prep)
        g = g_ref[rows, :]
        kk = kk_ref[rows, :].astype(F32)
        a = cumsum(g)
        a_end = jnp.sum(g.reshape(cpb, cs, LANES), axis=1)
        ea = jnp.exp(a)
        qt_ref[rows, :] = (q_ref[rows, :] * ea).astype(BF16)
        kt_ref[rows, :] = (kk * (1.0 / ea)).astype(BF16)
        to_end = jnp.exp(a_end[:, None, :] - a.reshape(cpb, cs, LANES)).reshape(prep, LANES)
        ke_ref[rows, :] = (kk * to_end).astype(BF16)
        dec_ref[pl.ds(pl.multiple_of(i * cpb, cpb), cpb), :] = jnp.exp(a_end)
        return carry

    lax.fori_loop(0, seq // prep, prepare, 0)


def _hgrn_chunks(directions, vb_ref, vt_ref, *, seq):
    cs = HGRN_CHUNK
    n_chunks = seq // cs
    unroll = math.gcd(HGRN_UNROLL // len(directions), n_chunks)
    head0 = lax.broadcasted_iota(jnp.int32, (1, LANES), 1) < HEAD_DIM
    same_head = _head_block_ones()
    t_h = lax.broadcasted_iota(jnp.int32, (2 * cs, cs), 0) % cs
    s_h = lax.broadcasted_iota(jnp.int32, (2 * cs, cs), 1)

    def step(i, states_in):
        plans = []
        for (qt_ref, kt_ref, ke_ref, dec_ref, oacc_ref, reverse), st in zip(directions, states_in):
            chunks = [i * unroll + u for u in range(unroll)]
            if reverse:
                chunks = [n_chunks - 1 - j for j in chunks]
            rows = [pl.ds(pl.multiple_of(c * cs, cs), cs) for c in chunks]
            upds = [jnp.dot(vt_ref[pl.ds(pl.multiple_of(c * LANES, LANES), LANES), :],
                            ke_ref[r, :], preferred_element_type=F32)
                    for c, r in zip(chunks, rows)]
            atts = [lax.dot_general(_split_heads(qt_ref[r, :], head0), kt_ref[r, :],
                                    (((1,), (1,)), ((), ())), preferred_element_type=F32)
                    for r in rows]
            plans.append((chunks, rows, upds, atts, st))
        states_out = []
        for (qt_ref, kt_ref, ke_ref, dec_ref, oacc_ref, reverse), plan in zip(directions, plans):
            chunks, rows, upds, atts, st = plan
            keep = (s_h >= t_h) if reverse else (s_h <= t_h)
            states = []
            for c, upd in zip(chunks, upds):
                states.append(st.astype(BF16))
                st = jnp.where(same_head, st * dec_ref[pl.ds(c, 1), :] + upd, 0.0)
            states_out.append(st)
            for r, att, st_in in zip(rows, atts, states):
                att = jnp.where(keep, att, 0.0).astype(BF16)
                both = jnp.dot(att, vb_ref[r, :], preferred_element_type=F32)
                oacc_ref[r, :] = (jnp.where(head0, both[0:cs], both[cs:])
                                  + lax.dot_general(qt_ref[r, :], st_in, (((1,), (1,)), ((), ())),
                                                    preferred_element_type=F32))
        return tuple(states_out)

    lax.fori_loop(0, n_chunks // unroll, step,
                  tuple(jnp.zeros((LANES, LANES), F32) for _ in directions))


def _hgrn_safe(q_ref, g_ref, kk_ref, vb_ref, oacc_ref, *, seq, reverse):
    cs = HGRN_CHUNK
    n_chunks = seq // cs
    head0 = lax.broadcasted_iota(jnp.int32, (1, LANES), 1) < HEAD_DIM
    same_head = _head_block_ones()
    keep = _causal_keep(reverse)
    col = lax.broadcasted_iota(jnp.int32, (LANES, LANES), 1)
    last = 0 if reverse else cs - 1

    def chunk(j, st):
        c = n_chunks - 1 - j if reverse else j
        rows = pl.ds(pl.multiple_of(c * cs, cs), cs)
        g = g_ref[rows, :]
        kk = kk_ref[rows, :].astype(F32)
        q = q_ref[rows, :]
        vb = vb_ref[rows, :]
        a = _chunk_cumsum(g, reverse)
        a_end = a[last:last + 1, :]
        qt = (q * jnp.exp(a)).astype(BF16)

        def column(s, att):
            sel = lax.broadcasted_iota(jnp.int32, (cs, 1), 0) == s
            a_s = jnp.sum(jnp.where(sel, a, 0.0), axis=0, keepdims=True)
            kk_s = jnp.sum(jnp.where(sel, kk, 0.0), axis=0, keepdims=True)
            e = jnp.exp(jnp.minimum(a - a_s, 0.0)) * (q * kk_s)
            place = jnp.logical_and(same_head, col % HEAD_DIM == s).astype(F32)
            return att + jnp.dot(e, place, preferred_element_type=F32,
                                 precision=lax.Precision.HIGHEST)

        att = lax.fori_loop(0, cs, column, jnp.zeros((cs, LANES), F32))
        att = jnp.where(keep, att, 0.0).astype(BF16)
        o = (jnp.dot(att, _split_heads(vb, head0), preferred_element_type=F32)
             + lax.dot_general(qt, st.astype(BF16), (((1,), (1,)), ((), ())),
                               preferred_element_type=F32))
        k_end = (kk * jnp.exp(a_end - a)).astype(BF16)
        upd = lax.dot_general(vb, k_end, (((0,), (0,)), ((), ())), preferred_element_type=F32)
        st = jnp.where(same_head, st * jnp.exp(a_end) + upd, 0.0)
        oacc_ref[rows, :] = o
        return st

    lax.fori_loop(0, n_chunks, chunk, jnp.zeros((LANES, LANES), F32))


def _hgrn_body(q_ref, gf_ref, gb_ref, kf_ref, kb_ref, vb_ref, gate_ref, lbf_ref, lbb_ref, nw_ref,
               out_ref, of_ref, ob_ref, vt_ref, qtf_ref, ktf_ref, kef_ref, decf_ref,
               qtb_ref, ktb_ref, keb_ref, decb_ref, *, seq):
    rows_n = HGRN_PREP_ROWS if seq % HGRN_PREP_ROWS == 0 else LANES

    def transpose_v(i, carry):
        for t in range(rows_n // HGRN_CHUNK):
            src = pl.ds(pl.multiple_of(i * rows_n + t * HGRN_CHUNK, HGRN_CHUNK), HGRN_CHUNK)
            dst = pl.ds(pl.multiple_of(2 * i * rows_n + t * LANES, LANES), LANES)
            vt_ref[dst, :] = vb_ref[src, :].astype(F32).T.astype(BF16)
        return carry

    worst = jnp.maximum(jnp.max(-jnp.log(_lower_bound(lbf_ref))),
                        jnp.max(-jnp.log(_lower_bound(lbb_ref)))) * HGRN_CHUNK

    @pl.when(worst <= F32_EXP_SAFE)
    def _():
        lax.fori_loop(0, seq // rows_n, transpose_v, 0)
        _hgrn_prepare(q_ref, gf_ref, kf_ref, qtf_ref, ktf_ref, kef_ref, decf_ref,
                      seq=seq, reverse=False)
        _hgrn_prepare(q_ref, gb_ref, kb_ref, qtb_ref, ktb_ref, keb_ref, decb_ref,
                      seq=seq, reverse=True)
        _hgrn_chunks([(qtf_ref, ktf_ref, kef_ref, decf_ref, of_ref, False),
                      (qtb_ref, ktb_ref, keb_ref, decb_ref, ob_ref, True)],
                     vb_ref, vt_ref, seq=seq)

    @pl.when(jnp.logical_not(worst <= F32_EXP_SAFE))
    def _():
        _hgrn_safe(q_ref, gf_ref, kf_ref, vb_ref, of_ref, seq=seq, reverse=False)
        _hgrn_safe(q_ref, gb_ref, kb_ref, vb_ref, ob_ref, seq=seq, reverse=True)

    ones_blk = _head_block_ones().astype(BF16)

    def finish(i, carry):
        rows = pl.ds(pl.multiple_of(i * rows_n, rows_n), rows_n)
        o = of_ref[rows, :] + ob_ref[rows, :]
        sq = o * o
        hi = sq.astype(BF16)
        lo = (sq - hi.astype(F32)).astype(BF16)
        ms = (jnp.dot(hi, ones_blk, preferred_element_type=F32)
              + jnp.dot(lo, ones_blk, preferred_element_type=F32)) * (1.0 / HEAD_DIM)
        y = o * lax.rsqrt(ms + NORM_EPS) * nw_ref[...] * gate_ref[rows, :].astype(F32)
        out_ref[rows, :] = y.astype(BF16)
        return carry

    lax.fori_loop(0, seq // rows_n, finish, 0)


def _hgrn(hf, hb, lb_fwd, lb_bwd, out_norm_w, batch, seq):
    nslots = lb_fwd.shape[0]
    cols = HGRN_WIDTH // LANES

    def slab(k):
        return pl.BlockSpec((None, None, seq, LANES), lambda b, p, k=k: (b, k * cols + p, 0, 0))

    return pl.pallas_call(
        functools.partial(_hgrn_body, seq=seq),
        grid=(batch, cols),
        in_specs=[slab(0), slab(1), slab(2), slab(0), slab(1), slab(2), slab(3),
                  pl.BlockSpec((nslots, LANES), lambda b, p: (0, p)),
                  pl.BlockSpec((nslots, LANES), lambda b, p: (0, p)),
                  pl.BlockSpec((1, LANES), lambda b, p: (0, 0))],
        out_specs=slab(0),
        out_shape=jax.ShapeDtypeStruct((batch, cols, seq, LANES), BF16),
        scratch_shapes=([pltpu.VMEM((seq, LANES), F32),
                         pltpu.VMEM((seq, LANES), F32),
                         pltpu.VMEM((seq // HGRN_CHUNK * LANES, HGRN_CHUNK), BF16)]
                        + 2 * [pltpu.VMEM((seq, LANES), BF16),
                               pltpu.VMEM((seq, LANES), BF16),
                               pltpu.VMEM((seq, LANES), BF16),
                               pltpu.VMEM((seq // HGRN_CHUNK, LANES), F32)]),
        compiler_params=_cparams(("arbitrary", "arbitrary")),
        name="hgrn",
    )(hf, hf, hf, hb, hb, hb, hb, lb_fwd, lb_bwd, out_norm_w)


FF_CHUNK = 256
HALO = 16


def _mix_ffn_body(ap_ref, a_ref, an_ref, gp_ref, g_ref, gn_ref, xp_ref, x_ref, xn_ref,
                  wo_ref, nw_mix_ref, nw_pre_ref, wg_ref, wu_ref, cw_ref, cb_ref, wd_ref,
                  nw_post_ref, y_ref, act_ref, *, tm, tiles_per_seq):
    i = pl.program_id(0)
    ext = tm + 2 * HALO

    def mixer_rows(attn_ref, hgrn_ref):
        return jnp.concatenate([attn_ref[j] for j in range(attn_ref.shape[0])]
                               + [hgrn_ref[j] for j in range(hgrn_ref.shape[0])], axis=1)

    mixed = jnp.concatenate([mixer_rows(ap_ref, gp_ref), mixer_rows(a_ref, g_ref),
                             mixer_rows(an_ref, gn_ref)], axis=0)
    mix = jnp.dot(mixed, wo_ref[...], preferred_element_type=F32)
    x_ext = jnp.concatenate([xp_ref[...], x_ref[...], xn_ref[...]], axis=0)
    h_ext = x_ext + mix * _rms_scale(mix) * nw_mix_ref[...]
    h = h_ext[HALO:HALO + tm]

    row = lax.broadcasted_iota(jnp.int32, (ext, 1), 0)
    has_prev = (i % tiles_per_seq != 0).astype(F32)
    has_next = (i % tiles_per_seq != tiles_per_seq - 1).astype(F32)
    live = jnp.where(row < HALO, has_prev, jnp.where(row >= HALO + tm, has_next, 1.0))
    xn = h_ext * _rms_scale(h_ext) * nw_pre_ref[...] * live
    xn_ext = xn.astype(BF16)
    xn_main = xn[HALO:HALO + tm].astype(BF16)
    for c in range(D_FF // FF_CHUNK):
        cols = slice(c * FF_CHUNK, (c + 1) * FF_CHUNK)
        a = jnp.dot(xn_ext, wg_ref[:, cols], preferred_element_type=F32)
        a_prev = pltpu.roll(a, 1, 0)[HALO:HALO + tm]
        a_next = pltpu.roll(a, ext - 1, 0)[HALO:HALO + tm]
        a_mid = a[HALO:HALO + tm]
        conv = (a_prev * cw_ref[0:1, cols] + a_mid * cw_ref[1:2, cols]
                + a_next * cw_ref[2:3, cols] + cb_ref[:, cols])
        b = jnp.dot(xn_main, wu_ref[:, cols], preferred_element_type=F32)
        gelu = 0.5 * conv * (1.0 + jnp.tanh(math.sqrt(2.0 / math.pi)
                                            * (conv + 0.044715 * (conv * conv * conv))))
        act_ref[:, cols] = (gelu * b).astype(BF16)
    ffn = jnp.dot(act_ref[...], wd_ref[...], preferred_element_type=F32)
    y_ref[...] = h + ffn * _rms_scale(ffn) * nw_post_ref[...]


def _mix_ffn(attn, hg, x2d, p, seq, tm):
    tokens = x2d.shape[0]
    tiles_per_seq = seq // tm
    hb = tm // HALO
    halos_per_seq = seq // HALO
    n_halo = tokens // HALO
    resident = dict(pipeline_mode=pl.Buffered(1))

    def slabs(rows, row_block):
        return pl.BlockSpec((None, HEAD_PAIRS, rows, LANES),
                            lambda i: (i // tiles_per_seq, 0, row_block(i % tiles_per_seq), 0))

    def prev_block(t):
        return jnp.maximum(t * hb - 1, 0)

    def next_block(t):
        return jnp.minimum((t + 1) * hb, halos_per_seq - 1)

    head_group = [slabs(HALO, prev_block), slabs(tm, lambda t: t), slabs(HALO, next_block)]
    row_vec = lambda n: pl.BlockSpec((1, n), lambda i: (0, 0))
    return pl.pallas_call(
        functools.partial(_mix_ffn_body, tm=tm, tiles_per_seq=tiles_per_seq),
        grid=(tokens // tm,),
        in_specs=head_group + head_group + [
            pl.BlockSpec((HALO, D_MODEL), lambda i: (jnp.maximum(i * hb - 1, 0), 0)),
            pl.BlockSpec((tm, D_MODEL), lambda i: (i, 0)),
            pl.BlockSpec((HALO, D_MODEL), lambda i: (jnp.minimum((i + 1) * hb, n_halo - 1), 0)),
            pl.BlockSpec((D_MODEL, D_MODEL), lambda i: (0, 0), **resident),
            row_vec(D_MODEL),
            row_vec(D_MODEL),
            pl.BlockSpec((D_MODEL, D_FF), lambda i: (0, 0), **resident),
            pl.BlockSpec((D_MODEL, D_FF), lambda i: (0, 0), **resident),
            pl.BlockSpec((3, D_FF), lambda i: (0, 0)),
            row_vec(D_FF),
            pl.BlockSpec((D_FF, D_MODEL), lambda i: (0, 0), **resident),
            row_vec(D_MODEL),
        ],
        out_specs=pl.BlockSpec((tm, D_MODEL), lambda i: (i, 0)),
        out_shape=jax.ShapeDtypeStruct((tokens, D_MODEL), F32),
        scratch_shapes=[pltpu.VMEM((tm, D_FF), BF16)],
        compiler_params=_cparams(("arbitrary",)),
        name="mix_ffn",
    )(attn, attn, attn, hg, hg, hg, x2d, x2d, x2d, p["w_out"], p["norm_mix_post"],
      p["norm_ffn_pre"], p["w_gate"], p["w_up"], p["conv_w"], p["conv_b"], p["w_down"],
      p["norm_ffn_post"])


def _token_tile(seq):
    return 512 if seq % 512 == 0 else seq


def _encode(x, p):
    batch, seq, _ = x.shape
    tm = _token_tile(seq)
    x2d = x.reshape(batch * seq, D_MODEL)
    qkv_orders, hf, hb = _inproj(x2d, p["norm_mix_pre"], p["w_in"], _rotary_tables(seq),
                                 p["lb_fwd"], p["lb_bwd"], batch, seq, tm)
    attn = _attention(qkv_orders, batch, seq)
    hg = _hgrn(hf, hb, p["lb_fwd"], p["lb_bwd"], p["hgrn_out_norm"], batch, seq)
    y = _mix_ffn(attn, hg, x2d, p, seq, tm)
    return y.reshape(batch, seq, D_MODEL)


def kernel(x_prompt, x_sample, norm_mix_pre, w_in, hgrn_lb_fwd, hgrn_lb_bwd, hgrn_out_norm, w_out,
           norm_mix_post, norm_ffn_pre, w_gate, w_up, conv_w, conv_b, w_down, norm_ffn_post):
    assert w_in.shape[0] == 1, "one layer"
    p = {
        "norm_mix_pre": norm_mix_pre[0][None, :],
        "w_in": w_in[0].astype(BF16),
        "lb_fwd": hgrn_lb_fwd.astype(F32),
        "lb_bwd": hgrn_lb_bwd.astype(F32),
        "hgrn_out_norm": jnp.tile(hgrn_out_norm[0], LANES // HEAD_DIM)[None, :],
        "w_out": w_out[0].astype(BF16),
        "norm_mix_post": norm_mix_post[0][None, :],
        "norm_ffn_pre": norm_ffn_pre[0][None, :],
        "w_gate": w_gate[0].astype(BF16),
        "w_up": w_up[0].astype(BF16),
        "conv_w": conv_w[0],
        "conv_b": conv_b[0][None, :],
        "w_down": w_down[0].astype(BF16),
        "norm_ffn_post": norm_ffn_post[0][None, :],
    }
    return (_encode(x_prompt, p), _encode(x_sample, p))
```
